```python
import jax, jax.numpy as jnp
from jax import lax
import numpy as np

D_MODEL = 1024
BATCH = 2
SEQ = 8192
DEPTH = 4

N_A_LAYERS = DEPTH // 2
N_B_LAYERS = DEPTH - N_A_LAYERS
N_DENSE = (DEPTH + 1) // 2
N_MOE = DEPTH // 2
HGRN_EXPAND = 128
HGRN_HEADS = D_MODEL // HGRN_EXPAND
HGRN_DV = D_MODEL // HGRN_HEADS
HGRN_CHUNK = 64
SB_HEADS = 16
SB_HEAD_DIM = D_MODEL // SB_HEADS
SB_BLOCK = 128
FFN_DENSE = 2816
N_EXPERTS = 8
TOP_K = 2
FFN_EXPERT = 3584
MOE_BLOCK = 128
EPS = 1e-6
F_FLOOR = 1e-30

kernel_name = "yoco_hgrn2_stickbreak_moe"


def rmsnorm(x, w):
    xf = x.astype(jnp.float32)
    y = xf * lax.rsqrt(jnp.mean(xf * xf, axis=-1, keepdims=True) + EPS)
    return (y * w.astype(jnp.float32)).astype(x.dtype)


def hgrn2_lower_bounds(lb_raw):
    p = jax.nn.softmax(lb_raw.astype(jnp.float32), axis=0)
    return jnp.cumsum(p, axis=0) - p[0:1]


def hgrn2_mixer(h, w_in, lb, gnorm_w, w_out):
    bsz, seq, _ = h.shape
    n_chunks = seq // HGRN_CHUNK
    q, f_raw, i, g = jnp.split(h @ w_in, 4, axis=-1)
    q = jax.nn.silu(q.astype(jnp.float32))
    f_raw = f_raw.astype(jnp.float32)
    f = lb + (1.0 - lb) * jax.nn.sigmoid(f_raw)
    log_f = jnp.log(jnp.maximum(f, F_FLOOR))
    k = (1.0 - lb) * jax.nn.sigmoid(-f_raw)

    def to_chunks(t):
        return t.reshape(bsz, n_chunks, HGRN_CHUNK, HGRN_HEADS, -1).transpose(1, 0, 3, 2, 4)

    qc, kc, vc, lfc = (to_chunks(t) for t in (q, k, i.astype(jnp.float32), log_f))
    causal = jnp.tril(jnp.ones((HGRN_CHUNK, HGRN_CHUNK), dtype=bool))[:, :, None]

    def step(state, inp):
        qt, kt, vt, lft = inp
        G = jnp.cumsum(lft, axis=2)
        o = jnp.einsum('bhtd,bhde->bhte', qt * jnp.exp(G), state)
        diff = G[:, :, :, None, :] - G[:, :, None, :, :]
        decay = jnp.where(causal, jnp.exp(jnp.where(causal, diff, 0.0)), 0.0)
        scores = jnp.einsum('bhtd,bhtsd,bhsd->bhts', qt, decay, kt)
        o = o + jnp.einsum('bhts,bhse->bhte', scores, vt)
        G_last = G[:, :, -1:, :]
        state = (jnp.exp(G_last[:, :, 0, :])[..., None] * state
                 + jnp.einsum('bhsd,bhse->bhde', kt * jnp.exp(G_last - G), vt))
        return state, o

    state0 = jnp.zeros((bsz, HGRN_HEADS, HGRN_EXPAND, HGRN_DV), jnp.float32)
    _, o = lax.scan(step, state0, (qc, kc, vc, lfc))
    o = o.transpose(1, 0, 3, 2, 4).reshape(bsz, seq, HGRN_HEADS, HGRN_DV)
    o = o * lax.rsqrt(jnp.mean(o * o, axis=-1, keepdims=True) + EPS) * gnorm_w.astype(jnp.float32)
    o = o.reshape(bsz, seq, D_MODEL) * jax.nn.silu(g.astype(jnp.float32))
    return o.astype(h.dtype) @ w_out


def stick_breaking_attention(h, w_q, w_o, k, v):
    bsz, seq, _ = h.shape
    q = (h @ w_q).reshape(bsz, seq, SB_HEADS, SB_HEAD_DIM)
    scale = SB_HEAD_DIM ** -0.5
    outs = []
    for blk in range(seq // SB_BLOCK):
        q0 = blk * SB_BLOCK
        end = q0 + SB_BLOCK
        qb, kb, vb = q[:, q0:end], k[:, :end], v[:, :end]
        z = jnp.einsum('bqhd,bkhd->bhqk', qb, kb).astype(jnp.float32) * scale
        t_pos = q0 + jnp.arange(SB_BLOCK)
        s_pos = jnp.arange(end)
        mask = s_pos[None, :] < t_pos[:, None]
        log_beta = jax.nn.log_sigmoid(z)
        log_rem = jnp.where(mask, log_beta - z, 0.0)
        between = lax.cumsum(log_rem, axis=3, reverse=True) - log_rem
        weights = jnp.where(mask, jnp.exp(jnp.where(mask, log_beta + between, 0.0)), 0.0)
        outs.append(jnp.einsum('bhqk,bkhd->bqhd', weights.astype(vb.dtype), vb))
    o = jnp.concatenate(outs, axis=1).reshape(bsz, seq, D_MODEL)
    return o @ w_o


def swiglu(h, w_gu, w_down):
    gate, up = jnp.split(h @ w_gu, 2, axis=-1)
    return (jax.nn.silu(gate) * up) @ w_down


def moe_swiglu(h, router, w_gate, w_up, w_down):
    bsz, seq, d = h.shape
    n_tok = bsz * seq
    n_assign = n_tok * TOP_K
    xf = h.reshape(n_tok, d)
    logits = (xf @ router).astype(jnp.float32)
    top_logits, top_idx = lax.top_k(logits, TOP_K)
    gates = jax.nn.softmax(top_logits, axis=-1)
    flat_e = top_idx.reshape(-1)
    flat_tok = jnp.repeat(jnp.arange(n_tok, dtype=jnp.int32), TOP_K)
    flat_gate = gates.reshape(-1).astype(h.dtype)
    order = jnp.argsort(flat_e)
    sorted_e = flat_e[order]
    counts = jnp.bincount(flat_e, length=N_EXPERTS)
    starts = jnp.cumsum(counts) - counts
    padded = (counts + MOE_BLOCK - 1) // MOE_BLOCK * MOE_BLOCK
    pad_ends = jnp.cumsum(padded)
    pad_starts = pad_ends - padded
    dest = pad_starts[sorted_e] + jnp.arange(n_assign) - starts[sorted_e]
    n_rows = n_assign + N_EXPERTS * MOE_BLOCK
    n_blocks = n_rows // MOE_BLOCK
    buf_tok = jnp.full((n_rows,), n_tok, jnp.int32).at[dest].set(flat_tok[order])
    buf_gate = jnp.zeros((n_rows,), h.dtype).at[dest].set(flat_gate[order])
    block_expert = jnp.minimum(
        jnp.searchsorted(pad_ends, jnp.arange(n_blocks) * MOE_BLOCK, side='right'), N_EXPERTS - 1)
    x_pad = jnp.concatenate([xf, jnp.zeros((1, d), h.dtype)], axis=0)
    xb = x_pad[buf_tok].reshape(n_blocks, MOE_BLOCK, d)

    def expert_block(args):
        xblk, e = args
        return (jax.nn.silu(xblk @ w_gate[e]) * (xblk @ w_up[e])) @ w_down[e]

    yb = lax.map(expert_block, (xb, block_expert))
    y = yb.reshape(n_rows, d) * buf_gate[:, None]
    out = jnp.zeros((n_tok + 1, d), y.dtype).at[buf_tok].add(y)[:n_tok]
    return out.reshape(bsz, seq, d)


def setup_inputs(seed: int = 0) -> dict:
    key = jax.random.key(seed)
    ks = jax.random.split(key, 18)

    def dense(k, shape, fan_in):
        return jax.random.normal(k, shape, jnp.float32) * fan_in ** -0.5

    def gain(k, shape):
        return 1.0 + 0.02 * jax.random.normal(k, shape, jnp.float32)

    D = D_MODEL
    return {
        "x": jax.random.normal(ks[0], (BATCH, SEQ, D), jnp.float32),
        "mix_norm": gain(ks[1], (DEPTH, D)),
        "ffn_norm": gain(ks[2], (DEPTH, D)),
        "hgrn_w_in": dense(ks[3], (N_A_LAYERS, D, 4 * D), D),
        "hgrn_lb_raw": 0.1 * jax.random.normal(ks[4], (N_A_LAYERS, D), jnp.float32),
        "hgrn_gnorm": gain(ks[5], (N_A_LAYERS, HGRN_DV)),
        "hgrn_w_out": dense(ks[6], (N_A_LAYERS, D, D), D),
        "kv_norm": gain(ks[7], (D,)),
        "w_kv": dense(ks[8], (D, 2 * D), D),
        "sb_w_q": dense(ks[9], (N_B_LAYERS, D, D), D),
        "sb_w_o": dense(ks[10], (N_B_LAYERS, D, D), D),
        "ffn_w_gu": dense(ks[11], (N_DENSE, D, 2 * FFN_DENSE), D),
        "ffn_w_down": dense(ks[12], (N_DENSE, FFN_DENSE, D), FFN_DENSE),
        "moe_router": dense(ks[13], (N_MOE, D, N_EXPERTS), D),
        "moe_w_gate": dense(ks[14], (N_MOE, N_EXPERTS, D, FFN_EXPERT), D),
        "moe_w_up": dense(ks[15], (N_MOE, N_EXPERTS, D, FFN_EXPERT), D),
        "moe_w_down": dense(ks[16], (N_MOE, N_EXPERTS, FFN_EXPERT, D), FFN_EXPERT),
        "final_norm": gain(ks[17], (D,)),
    }


def reference(x, mix_norm, ffn_norm, hgrn_w_in, hgrn_lb_raw, hgrn_gnorm, hgrn_w_out,
              kv_norm, w_kv, sb_w_q, sb_w_o, ffn_w_gu, ffn_w_down,
              moe_router, moe_w_gate, moe_w_up, moe_w_down, final_norm):
    bsz, seq, _ = x.shape
    lower_bounds = hgrn2_lower_bounds(hgrn_lb_raw)
    shared_k = None
    shared_v = None
    for layer in range(DEPTH):
        h = rmsnorm(x, mix_norm[layer])
        if layer < N_A_LAYERS:
            x = x + hgrn2_mixer(h, hgrn_w_in[layer], lower_bounds[layer],
                                hgrn_gnorm[layer], hgrn_w_out[layer])
        else:
            j = layer - N_A_LAYERS
            x = x + stick_breaking_attention(h, sb_w_q[j], sb_w_o[j], shared_k, shared_v)
        h = rmsnorm(x, ffn_norm[layer])
        if layer % 2 == 0:
            x = x + swiglu(h, ffn_w_gu[layer // 2], ffn_w_down[layer // 2])
        else:
            x = x + moe_swiglu(h, moe_router[layer // 2], moe_w_gate[layer // 2],
                               moe_w_up[layer // 2], moe_w_down[layer // 2])
        if layer == N_A_LAYERS - 1:
            kv = rmsnorm(x, kv_norm) @ w_kv
            k_flat, v_flat = jnp.split(kv, 2, axis=-1)
            shared_k = k_flat.reshape(bsz, seq, SB_HEADS, SB_HEAD_DIM)
            shared_v = v_flat.reshape(bsz, seq, SB_HEADS, SB_HEAD_DIM)
    return rmsnorm(x, final_norm)
```

```python
import functools
import math

import jax
import jax.numpy as jnp
from jax import lax
from jax.experimental import pallas as pl
from jax.experimental.pallas import tpu as pltpu

F32 = jnp.float32
BF16 = jnp.bfloat16

D_MODEL = 1024
HGRN_HEADS = 8
HGRN_DK = 128
HGRN_CHUNK = 64
SB_HEADS = 16
SB_HEAD_DIM = 64
N_EXPERTS = 8
TOP_K = 2
EPS = 1e-6
F_FLOOR = 1e-30
LOG2E = 1.4426950408889634

VMEM_LIMIT_BYTES = 52 * 1024 * 1024
EXP2_ZERO_BELOW = -150.0


def _cparams(sem):
    return pltpu.CompilerParams(dimension_semantics=sem, vmem_limit_bytes=VMEM_LIMIT_BYTES)


def _sigmoid(x):
    return 1.0 / (1.0 + jnp.exp(-x))


def _rmsnorm_rows(x, w):
    ms = jnp.mean(x * x, axis=-1, keepdims=True)
    return x * lax.rsqrt(ms + EPS) * w


def _norm_matmul_kernel(x_ref, nw_ref, w_ref, o_ref, h_ref):
    @pl.when(pl.program_id(1) == 0)
    def _():
        h_ref[...] = _rmsnorm_rows(x_ref[...], nw_ref[...]).astype(BF16)

    o_ref[...] = jnp.dot(h_ref[...], w_ref[...], preferred_element_type=F32).astype(o_ref.dtype)


def norm_matmul(x, norm_w, w_bf16, out_dtype, tm=1024, tn=1024):
    n, d = x.shape
    n_out = w_bf16.shape[1]
    return pl.pallas_call(
        _norm_matmul_kernel,
        grid=(n // tm, n_out // tn),
        in_specs=[
            pl.BlockSpec((tm, d), lambda i, j: (i, 0)),
            pl.BlockSpec((1, d), lambda i, j: (0, 0)),
            pl.BlockSpec((d, tn), lambda i, j: (0, j)),
        ],
        out_specs=pl.BlockSpec((tm, tn), lambda i, j: (i, j)),
        out_shape=jax.ShapeDtypeStruct((n, n_out), out_dtype),
        scratch_shapes=[pltpu.VMEM((tm, d), BF16)],
        compiler_params=_cparams(("parallel", "arbitrary")),
        name="norm_matmul",
    )(x, norm_w.reshape(1, d), w_bf16)


def _matmul_res_kernel(a_ref, w_ref, r_ref, o_ref):
    o_ref[...] = r_ref[...] + jnp.dot(a_ref[...], w_ref[...], preferred_element_type=F32)


def matmul_residual(a_bf16, w_bf16, res, tm=1024, tn=1024):
    n, k = a_bf16.shape
    n_out = w_bf16.shape[1]
    return pl.pallas_call(
        _matmul_res_kernel,
        grid=(n // tm, n_out // tn),
        in_specs=[
            pl.BlockSpec((tm, k), lambda i, j: (i, 0)),
            pl.BlockSpec((k, tn), lambda i, j: (0, j)),
            pl.BlockSpec((tm, tn), lambda i, j: (i, j)),
        ],
        out_specs=pl.BlockSpec((tm, tn), lambda i, j: (i, j)),
        out_shape=jax.ShapeDtypeStruct((n, n_out), F32),
        compiler_params=_cparams(("parallel", "parallel")),
        name="matmul_residual",
    )(a_bf16, w_bf16, res)


def _cumsum_rows(x, row):
    n = x.shape[0]
    s = 1
    while s < n:
        x = x + jnp.where(row >= s, pltpu.roll(x, s, axis=0), 0.0)
        s *= 2
    return x


def _hgrn_chunk(qr, fr, v, lb, state_t):
    c = HGRN_CHUNK
    q = qr * _sigmoid(qr)
    sig = _sigmoid(fr)
    f = lb + (1.0 - lb) * sig
    logf2 = jnp.log2(jnp.maximum(f, F_FLOOR))
    k = (1.0 - lb) * _sigmoid(-fr)

    row = lax.broadcasted_iota(jnp.int32, (c, HGRN_DK), 0)
    g = _cumsum_rows(logf2, row)

    qg = (q * jnp.exp2(g)).astype(BF16)
    o_inter = lax.dot_general(qg, state_t.astype(BF16), (((1,), (1,)), ((), ())),
                              preferred_element_type=F32)

    nv = c // 8
    g8 = [g[8 * j:8 * (j + 1), :] for j in range(nv)]
    q8 = [q[8 * j:8 * (j + 1), :] for j in range(nv)]
    acc8 = [o_inter[8 * j:8 * (j + 1), :] for j in range(nv)]
    row8 = lax.broadcasted_iota(jnp.int32, (8, HGRN_DK), 0)
    for s in range(c):
        j0, r = divmod(s, 8)
        gs = g8[j0][r:r + 1, :]
        ks = k[s:s + 1, :]
        vs = v[s:s + 1, :]
        for j in range(j0, nv):
            diff = g8[j] - gs
            if j == j0 and r > 0:
                m = row8 >= r
                p = jnp.where(m, q8[j] * jnp.exp2(jnp.where(m, diff, 0.0)) * ks, 0.0)
            else:
                p = q8[j] * jnp.exp2(diff) * ks
            cs = jnp.sum(p, axis=-1, keepdims=True)
            acc8[j] = acc8[j] + cs * vs
    o = jnp.concatenate(acc8, axis=0)

    g_last = g[c - 1:c, :]
    kd = (k * jnp.exp2(g_last - g)).astype(BF16)
    upd_t = lax.dot_general(v.astype(BF16), kd, (((0,), (0,)), ((), ())),
                            preferred_element_type=F32)
    new_state_t = jnp.exp2(g_last) * state_t + upd_t
    return o, new_state_t


def _hgrn_kernel(q_ref, f_ref, i_ref, g_ref, lb_ref, gw_ref, o_ref, state_ref, *, n_chunks):
    @pl.when(pl.program_id(2) == 0)
    def _():
        state_ref[...] = jnp.zeros_like(state_ref)

    lb = lb_ref[...]
    gw = gw_ref[...]

    def body(ci, carry):
        r0 = pl.multiple_of(ci * HGRN_CHUNK, HGRN_CHUNK)
        rows = pl.ds(r0, HGRN_CHUNK)
        o, new_state = _hgrn_chunk(q_ref[rows, :], f_ref[rows, :], i_ref[rows, :], lb, state_ref[...])
        state_ref[...] = new_state
        o = o * lax.rsqrt(jnp.mean(o * o, axis=-1, keepdims=True) + EPS) * gw
        gate = g_ref[rows, :]
        o_ref[rows, :] = (o * (gate * _sigmoid(gate))).astype(o_ref.dtype)
        return carry

    lax.fori_loop(0, n_chunks, body, 0)


def hgrn_recurrence(proj, lb, gnorm_w, bsz, seq, block_len=1024):
    n = bsz * seq
    nl = seq // block_len
    nh = HGRN_HEADS

    def col_spec(offset):
        return pl.BlockSpec((block_len, HGRN_DK), lambda b, h, l: (b * nl + l, offset + h))

    vec_spec = pl.BlockSpec((1, HGRN_DK), lambda b, h, l: (0, h))
    return pl.pallas_call(
        functools.partial(_hgrn_kernel, n_chunks=block_len // HGRN_CHUNK),
        grid=(bsz, nh, nl),
        in_specs=[col_spec(0), col_spec(nh), col_spec(2 * nh), col_spec(3 * nh), vec_spec,
                  pl.BlockSpec((1, HGRN_DK), lambda b, h, l: (0, 0))],
        out_specs=pl.BlockSpec((block_len, HGRN_DK), lambda b, h, l: (b * nl + l, h)),
        out_shape=jax.ShapeDtypeStruct((n, D_MODEL), BF16),
        scratch_shapes=[pltpu.VMEM((HGRN_DK, HGRN_DK), F32)],
        compiler_params=_cparams(("parallel", "parallel", "arbitrary")),
        name="hgrn_recurrence",
    )(proj, proj, proj, proj, lb.reshape(1, D_MODEL), gnorm_w.reshape(1, HGRN_DK))


def _sb_kernel(q_ref, k_ref, v_ref, o_ref, acc_ref, rem_ref, *, tq, tk):
    i = pl.program_id(2)
    scale2 = (SB_HEAD_DIM ** -0.5) * LOG2E
    q = q_ref[...]
    lane = lax.broadcasted_iota(jnp.int32, (tq, 2 * SB_HEAD_DIM), 1)
    first_head = lane < SB_HEAD_DIM
    q_heads = (jnp.where(first_head, q, jnp.zeros_like(q)), jnp.where(first_head, jnp.zeros_like(q), q))
    t_pos = i * tq + lax.broadcasted_iota(jnp.int32, (tq, tk), 0)
    s_loc = lax.broadcasted_iota(jnp.int32, (tq, tk), 1)
    later = (lax.broadcasted_iota(jnp.int32, (tk, tk), 0)
             > lax.broadcasted_iota(jnp.int32, (tk, tk), 1)).astype(BF16)

    acc_ref[...] = jnp.zeros_like(acc_ref)
    rem_ref[...] = jnp.zeros_like(rem_ref)

    def cond(carry):
        kb, rem_max = carry
        return jnp.logical_and(kb >= 0, rem_max > EXP2_ZERO_BELOW)

    def body(carry):
        kb, _ = carry
        k0 = pl.multiple_of(kb * tk, tk)
        kblk = k_ref[pl.ds(k0, tk), :]
        vblk = v_ref[pl.ds(k0, tk), :]
        mask = (k0 + s_loc) < t_pos
        rem_max = jnp.float32(-jnp.inf)
        for h in range(2):
            rem = rem_ref[h]
            z = lax.dot_general(q_heads[h], kblk, (((1,), (1,)), ((), ())),
                                preferred_element_type=F32) * scale2
            soft = jnp.log2(1.0 + jnp.exp2(-jnp.abs(z)))
            log_beta = jnp.minimum(z, 0.0) - soft
            log_rem = jnp.where(mask, log_beta - z, 0.0)
            hi = log_rem.astype(BF16)
            lo = (log_rem - hi.astype(F32)).astype(BF16)
            between = (jnp.dot(hi, later, preferred_element_type=F32)
                       + jnp.dot(lo, later, preferred_element_type=F32))
            w = jnp.where(mask, jnp.exp2(jnp.where(mask, log_beta + between + rem, 0.0)), 0.0)
            acc_ref[h] += jnp.dot(w.astype(BF16), vblk, preferred_element_type=F32)
            rem = rem + jnp.sum(log_rem, axis=-1, keepdims=True)
            rem_ref[h] = rem
            rem_max = jnp.maximum(rem_max, jnp.max(rem))
        return kb - 1, rem_max

    kb0 = ((i + 1) * tq) // tk - 1
    lax.while_loop(cond, body, (kb0, jnp.float32(0.0)))
    o_ref[...] = jnp.where(first_head, acc_ref[0], acc_ref[1]).astype(o_ref.dtype)


def stick_breaking(q, kv, bsz, seq, tq=256, tk=256):
    n = bsz * seq
    nq = seq // tq
    pairs = SB_HEADS // 2
    w = 2 * SB_HEAD_DIM
    return pl.pallas_call(
        functools.partial(_sb_kernel, tq=tq, tk=tk),
        grid=(bsz, pairs, nq),
        in_specs=[
            pl.BlockSpec((tq, w), lambda b, p, i: (b * nq + i, p)),
            pl.BlockSpec((seq, w), lambda b, p, i: (b, p)),
            pl.BlockSpec((seq, w), lambda b, p, i: (b, pairs + p)),
        ],
        out_specs=pl.BlockSpec((tq, w), lambda b, p, i: (b * nq + i, p)),
        out_shape=jax.ShapeDtypeStruct((n, D_MODEL), BF16),
        scratch_shapes=[pltpu.VMEM((2, tq, w), F32), pltpu.VMEM((2, tq, 1), F32)],
        compiler_params=_cparams(("parallel", "parallel", "arbitrary")),
        name="stick_breaking",
    )(q, kv, kv)


def _swiglu_kernel(x_ref, nw_ref, wg_ref, wu_ref, wd_ref, o_ref, h_ref, acc_ref):
    j = pl.program_id(1)

    @pl.when(j == 0)
    def _():
        x = x_ref[...]
        h_ref[...] = _rmsnorm_rows(x, nw_ref[...]).astype(BF16)
        acc_ref[...] = x

    h = h_ref[...]
    gate = jnp.dot(h, wg_ref[...], preferred_element_type=F32)
    up = jnp.dot(h, wu_ref[...], preferred_element_type=F32)
    act = (gate * _sigmoid(gate) * up).astype(BF16)
    acc_ref[...] += jnp.dot(act, wd_ref[...], preferred_element_type=F32)

    @pl.when(j == pl.num_programs(1) - 1)
    def _():
        o_ref[...] = acc_ref[...]


def swiglu_residual(x, norm_w, w_gu_bf16, w_down_bf16, tm=512, tf=1408):
    n, d = x.shape
    f = w_down_bf16.shape[0]
    nf = f // tf
    return pl.pallas_call(
        _swiglu_kernel,
        grid=(n // tm, nf),
        in_specs=[
            pl.BlockSpec((tm, d), lambda i, j: (i, 0)),
            pl.BlockSpec((1, d), lambda i, j: (0, 0)),
            pl.BlockSpec((d, tf), lambda i, j: (0, j)),
            pl.BlockSpec((d, tf), lambda i, j: (0, nf + j)),
            pl.BlockSpec((tf, d), lambda i, j: (j, 0)),
        ],
        out_specs=pl.BlockSpec((tm, d), lambda i, j: (i, 0)),
        out_shape=jax.ShapeDtypeStruct((n, d), F32),
        scratch_shapes=[pltpu.VMEM((tm, d), BF16), pltpu.VMEM((tm, d), F32)],
        compiler_params=_cparams(("parallel", "arbitrary")),
        name="swiglu_residual",
    )(x, norm_w.reshape(1, d), w_gu_bf16, w_gu_bf16, w_down_bf16)


ROUTER_LANES = 128


def _router_kernel(x_ref, nw_ref, r_ref, h_ref, l_ref):
    h = _rmsnorm_rows(x_ref[...], nw_ref[...])
    h_ref[...] = h.astype(BF16)
    l_ref[...] = jnp.dot(h, r_ref[...], preferred_element_type=F32, precision=lax.Precision.HIGHEST)


def router_logits(x, norm_w, router, tm=1024):
    n, d = x.shape
    r_pad = jnp.zeros((d, ROUTER_LANES), F32).at[:, :N_EXPERTS].set(router)
    return pl.pallas_call(
        _router_kernel,
        grid=(n // tm,),
        in_specs=[
            pl.BlockSpec((tm, d), lambda i: (i, 0)),
            pl.BlockSpec((1, d), lambda i: (0, 0)),
            pl.BlockSpec((d, ROUTER_LANES), lambda i: (0, 0)),
        ],
        out_specs=[pl.BlockSpec((tm, d), lambda i: (i, 0)),
                   pl.BlockSpec((tm, ROUTER_LANES), lambda i: (i, 0))],
        out_shape=[jax.ShapeDtypeStruct((n, d), BF16),
                   jax.ShapeDtypeStruct((n, ROUTER_LANES), F32)],
        compiler_params=_cparams(("parallel",)),
        name="moe_router",
    )(x, norm_w.reshape(1, d), r_pad)


def _moe_kernel(be_ref, nb_ref, x_ref, wg_ref, wu_ref, wd_ref, o_ref, acc_ref):
    i = pl.program_id(0)
    j = pl.program_id(1)
    last = pl.num_programs(1) - 1
    used = i < nb_ref[0]

    @pl.when(used)
    def _():
        x = x_ref[...]
        gate = jnp.dot(x, wg_ref[0], preferred_element_type=F32)
        up = jnp.dot(x, wu_ref[0], preferred_element_type=F32)
        act = (gate * _sigmoid(gate) * up).astype(BF16)
        part = jnp.dot(act, wd_ref[0], preferred_element_type=F32)

        @pl.when(j == 0)
        def _():
            acc_ref[...] = part

        @pl.when(j > 0)
        def _():
            acc_ref[...] += part

        @pl.when(j == last)
        def _():
            o_ref[...] = acc_ref[...]

    @pl.when(jnp.logical_and(jnp.logical_not(used), j == last))
    def _():
        o_ref[...] = jnp.zeros_like(o_ref)


def moe_experts(xb, block_expert, n_used, wg, wu, wd, tm, tf=896):
    n_rows, d = xb.shape
    f = wg.shape[2]
    n_blocks = n_rows // tm

    def x_map(i, j, be, nb):
        return (jnp.minimum(i, nb[0] - 1), 0)

    grid_spec = pltpu.PrefetchScalarGridSpec(
        num_scalar_prefetch=2,
        grid=(n_blocks, f // tf),
        in_specs=[
            pl.BlockSpec((tm, d), x_map),
            pl.BlockSpec((1, d, tf), lambda i, j, be, nb: (be[i], 0, j)),
            pl.BlockSpec((1, d, tf), lambda i, j, be, nb: (be[i], 0, j)),
            pl.BlockSpec((1, tf, d), lambda i, j, be, nb: (be[i], j, 0)),
        ],
        out_specs=pl.BlockSpec((tm, d), lambda i, j, be, nb: (i, 0)),
        scratch_shapes=[pltpu.VMEM((tm, d), F32)],
    )
    return pl.pallas_call(
        _moe_kernel,
        grid_spec=grid_spec,
        out_shape=jax.ShapeDtypeStruct((n_rows, d), F32),
        compiler_params=_cparams(("arbitrary", "arbitrary")),
        name="moe_experts",
    )(block_expert, n_used, xb, wg, wu, wd)


def moe_layer(x, norm_w, router, wg, wu, wd, tm=512):
    n, d = x.shape
    n_assign = n * TOP_K
    h, logits = router_logits(x, norm_w, router)
    top_logits, top_idx = lax.top_k(logits[:, :N_EXPERTS], TOP_K)
    gates = jax.nn.softmax(top_logits, axis=-1)

    flat_e = top_idx.reshape(-1).astype(jnp.int32)
    onehot = (flat_e[:, None] == jnp.arange(N_EXPERTS, dtype=jnp.int32)[None, :]).astype(jnp.int32)
    csum = jnp.cumsum(onehot, axis=0)
    rank = jnp.sum((csum - onehot) * onehot, axis=1)
    counts = csum[-1]
    padded = (counts + tm - 1) // tm * tm
    pad_ends = jnp.cumsum(padded)
    pad_starts = pad_ends - padded
    dest = pad_starts[flat_e] + rank

    n_rows = n_assign + N_EXPERTS * tm
    n_blocks = n_rows // tm
    flat_tok = jnp.repeat(jnp.arange(n, dtype=jnp.int32), TOP_K)
    buf_tok = jnp.full((n_rows,), n, jnp.int32).at[dest].set(flat_tok)
    block_expert = jnp.minimum(
        jnp.searchsorted(pad_ends, jnp.arange(n_blocks, dtype=jnp.int32) * tm, side="right"),
        N_EXPERTS - 1).astype(jnp.int32)
    n_used = (pad_ends[-1] // tm).astype(jnp.int32).reshape(1)

    h_pad = jnp.concatenate([h, jnp.zeros((1, d), h.dtype)], axis=0)
    xb = jnp.take(h_pad, buf_tok, axis=0)
    y = moe_experts(xb, block_expert, n_used, wg, wu, wd, tm)
    dest2 = dest.reshape(n, TOP_K)
    return x + gates[:, 0:1] * jnp.take(y, dest2[:, 0], axis=0) + gates[:, 1:2] * jnp.take(y, dest2[:, 1], axis=0)


def _rmsnorm_kernel(x_ref, w_ref, o_ref):
    o_ref[...] = _rmsnorm_rows(x_ref[...], w_ref[...])


def rmsnorm(x, w, tm=1024):
    n, d = x.shape
    return pl.pallas_call(
        _rmsnorm_kernel,
        grid=(n // tm,),
        in_specs=[pl.BlockSpec((tm, d), lambda i: (i, 0)), pl.BlockSpec((1, d), lambda i: (0, 0))],
        out_specs=pl.BlockSpec((tm, d), lambda i: (i, 0)),
        out_shape=jax.ShapeDtypeStruct((n, d), F32),
        compiler_params=_cparams(("parallel",)),
        name="final_rmsnorm",
    )(x, w.reshape(1, d))


def kernel(x, mix_norm, ffn_norm, hgrn_w_in, hgrn_lb_raw, hgrn_gnorm, hgrn_w_out, kv_norm, w_kv, sb_w_q, sb_w_o, ffn_w_gu, ffn_w_down, moe_router, moe_w_gate, moe_w_up, moe_w_down, final_norm):
    bsz, seq, d = x.shape
    n = bsz * seq
    depth = mix_norm.shape[0]
    n_a = hgrn_w_in.shape[0]
    xs = x.reshape(n, d)

    p = jax.nn.softmax(hgrn_lb_raw.astype(F32), axis=0)
    lower_bounds = jnp.cumsum(p, axis=0) - p[0:1]

    kv = None
    for layer in range(depth):
        if layer < n_a:
            proj = norm_matmul(xs, mix_norm[layer], hgrn_w_in[layer].astype(BF16), F32)
            og = hgrn_recurrence(proj, lower_bounds[layer], hgrn_gnorm[layer], bsz, seq)
            xs = matmul_residual(og, hgrn_w_out[layer].astype(BF16), xs)
        else:
            j = layer - n_a
            q = norm_matmul(xs, mix_norm[layer], sb_w_q[j].astype(BF16), BF16)
            att = stick_breaking(q, kv, bsz, seq)
            xs = matmul_residual(att, sb_w_o[j].astype(BF16), xs)
        if layer % 2 == 0:
            xs = swiglu_residual(xs, ffn_norm[layer], ffn_w_gu[layer // 2].astype(BF16),
                                 ffn_w_down[layer // 2].astype(BF16))
        else:
            e = layer // 2
            xs = moe_layer(xs, ffn_norm[layer], moe_router[e], moe_w_gate[e].astype(BF16),
                           moe_w_up[e].astype(BF16), moe_w_down[e].astype(BF16))
        if layer == n_a - 1:
            kv = norm_matmul(xs, kv_norm, w_kv.astype(BF16), BF16)
    return rmsnorm(xs, final_norm).reshape(bsz, seq, d)
```

```python
import functools

import jax
import jax.numpy as jnp
from jax import lax
from jax.experimental import pallas as pl
from jax.experimental.pallas import tpu as pltpu

F32 = jnp.float32
BF16 = jnp.bfloat16

D_MODEL = 1024
HGRN_HEADS = 8
HGRN_DK = 128
HGRN_CHUNK = 64
SB_HEADS = 16
SB_HEAD_DIM = 64
N_EXPERTS = 8
TOP_K = 2
EPS = 1e-6
F_FLOOR = 1e-30
LOG2E = 1.4426950408889634

VMEM_LIMIT_BYTES = 52 * 1024 * 1024
EXP2_ZERO_BELOW = -150.0
HGRN_MAX_FACTORED_SPAN = 100.0


def _cparams(sem):
    return pltpu.CompilerParams(dimension_semantics=sem, vmem_limit_bytes=VMEM_LIMIT_BYTES)


def _sigmoid(x):
    return 1.0 / (1.0 + jnp.exp(-x))


def _rmsnorm_rows(x, w):
    ms = jnp.mean(x * x, axis=-1, keepdims=True)
    return x * lax.rsqrt(ms + EPS) * w


def _dot_nt(a, b):
    return lax.dot_general(a, b, (((1,), (1,)), ((), ())), preferred_element_type=F32)


def _dot_tn(a, b):
    return lax.dot_general(a, b, (((0,), (0,)), ((), ())), preferred_element_type=F32)


def _norm_matmul_kernel(x_ref, nw_ref, w_ref, o_ref, h_ref):
    @pl.when(pl.program_id(1) == 0)
    def _():
        h_ref[...] = _rmsnorm_rows(x_ref[...], nw_ref[...]).astype(BF16)

    o_ref[...] = jnp.dot(h_ref[...], w_ref[...], preferred_element_type=F32).astype(o_ref.dtype)


def norm_matmul(x, norm_w, w_bf16, out_dtype, tm=1024, tn=1024):
    n, d = x.shape
    n_out = w_bf16.shape[1]
    return pl.pallas_call(
        _norm_matmul_kernel,
        grid=(n // tm, n_out // tn),
        in_specs=[
            pl.BlockSpec((tm, d), lambda i, j: (i, 0)),
            pl.BlockSpec((1, d), lambda i, j: (0, 0)),
            pl.BlockSpec((d, tn), lambda i, j: (0, j)),
        ],
        out_specs=pl.BlockSpec((tm, tn), lambda i, j: (i, j)),
        out_shape=jax.ShapeDtypeStruct((n, n_out), out_dtype),
        scratch_shapes=[pltpu.VMEM((tm, d), BF16)],
        compiler_params=_cparams(("parallel", "arbitrary")),
        name="norm_matmul",
    )(x, norm_w.reshape(1, d), w_bf16)


def _matmul_res_kernel(a_ref, w_ref, r_ref, o_ref):
    o_ref[...] = r_ref[...] + jnp.dot(a_ref[...], w_ref[...], preferred_element_type=F32)


def matmul_residual(a_bf16, w_bf16, res, tm=1024, tn=1024):
    n, k = a_bf16.shape
    n_out = w_bf16.shape[1]
    return pl.pallas_call(
        _matmul_res_kernel,
        grid=(n // tm, n_out // tn),
        in_specs=[
            pl.BlockSpec((tm, k), lambda i, j: (i, 0)),
            pl.BlockSpec((k, tn), lambda i, j: (0, j)),
            pl.BlockSpec((tm, tn), lambda i, j: (i, j)),
        ],
        out_specs=pl.BlockSpec((tm, tn), lambda i, j: (i, j)),
        out_shape=jax.ShapeDtypeStruct((n, n_out), F32),
        compiler_params=_cparams(("parallel", "parallel")),
        name="matmul_residual",
    )(a_bf16, w_bf16, res)


def _cumsum_rows(x, row):
    n = x.shape[0]
    s = 1
    while s < n:
        x = x + jnp.where(row >= s, pltpu.roll(x, s, axis=0), 0.0)
        s *= 2
    return x


def _hgrn_finish(o, gate_raw, gw):
    o = o * lax.rsqrt(jnp.mean(o * o, axis=-1, keepdims=True) + EPS) * gw
    return o * (gate_raw * _sigmoid(gate_raw))


def _hgrn_exact_head(q, k, v, g, state_t):
    c = HGRN_CHUNK
    qg = (q * jnp.exp2(g)).astype(BF16)
    o_inter = _dot_nt(qg, state_t.astype(BF16))
    nv = c // 8
    g8 = [g[8 * j:8 * (j + 1), :] for j in range(nv)]
    q8 = [q[8 * j:8 * (j + 1), :] for j in range(nv)]
    acc8 = [o_inter[8 * j:8 * (j + 1), :] for j in range(nv)]
    row8 = lax.broadcasted_iota(jnp.int32, (8, HGRN_DK), 0)
    for s in range(c):
        j0, r = divmod(s, 8)
        gs = g8[j0][r:r + 1, :]
        ks = k[s:s + 1, :]
        vs = v[s:s + 1, :]
        for j in range(j0, nv):
            diff = g8[j] - gs
            if j == j0 and r > 0:
                m = row8 >= r
                p = jnp.where(m, q8[j] * jnp.exp2(jnp.where(m, diff, 0.0)) * ks, 0.0)
            else:
                p = q8[j] * jnp.exp2(diff) * ks
            cs = jnp.sum(p, axis=-1, keepdims=True)
            acc8[j] = acc8[j] + cs * vs
    o = jnp.concatenate(acc8, axis=0)
    g_last = g[c - 1:c, :]
    kd = (k * jnp.exp2(g_last - g)).astype(BF16)
    new_state_t = jnp.exp2(g_last) * state_t + _dot_tn(v.astype(BF16), kd)
    return o, new_state_t


def _hgrn_kernel(q_ref, f_ref, i_ref, g_ref, lb_ref, gw_ref, o_ref, state_ref, qs_ref, ks_ref, gs_ref,
                 *, n_chunks):
    @pl.when(pl.program_id(1) == 0)
    def _():
        state_ref[...] = jnp.zeros_like(state_ref)

    c = HGRN_CHUNK
    half = c // 2
    dk = HGRN_DK
    lb = lb_ref[...]
    gw = gw_ref[...]
    row = lax.broadcasted_iota(jnp.int32, (c, D_MODEL), 0)
    mask_a = (lax.broadcasted_iota(jnp.int32, (half, half), 1)
              <= lax.broadcasted_iota(jnp.int32, (half, half), 0))
    mask_b = (lax.broadcasted_iota(jnp.int32, (half, c), 1)
              <= lax.broadcasted_iota(jnp.int32, (half, c), 0) + half)

    def body(ci, carry):
        rows = pl.ds(pl.multiple_of(ci * c, c), c)
        qr = q_ref[rows, :]
        fr = f_ref[rows, :]
        q = qr * _sigmoid(qr)
        f = lb + (1.0 - lb) * _sigmoid(fr)
        k = (1.0 - lb) * _sigmoid(-fr)
        g = _cumsum_rows(jnp.log2(jnp.maximum(f, F_FLOOR)), row)
        qs_ref[...] = q
        ks_ref[...] = k
        gs_ref[...] = g
        g_mid = g[half - 1:half, :]
        g_last = g[c - 1:c, :]
        span = jnp.maximum(jnp.max(-g_mid), jnp.max(g_mid - g_last))

        @pl.when(span <= HGRN_MAX_FACTORED_SPAN)
        def _():
            q = qs_ref[...]
            k = ks_ref[...]
            g = gs_ref[...]
            qg = (q * jnp.exp2(g)).astype(BF16)
            qb = (q[half:, :] * jnp.exp2(g[half:, :] - g_mid)).astype(BF16)
            ka = (k[:half, :] * jnp.exp2(-g[:half, :])).astype(BF16)
            kb = (k * jnp.exp2(g_mid - g)).astype(BF16)
            kd = (k * jnp.exp2(g_last - g)).astype(BF16)
            dl = jnp.exp2(g_last)
            vb = i_ref[rows, :].astype(BF16)
            gate_raw = g_ref[rows, :]
            for h in range(HGRN_HEADS):
                sl = slice(h * dk, (h + 1) * dk)
                st = state_ref[h]
                o_inter = _dot_nt(qg[:, sl], st.astype(BF16))
                sa = jnp.where(mask_a, _dot_nt(qg[:half, sl], ka[:, sl]), 0.0).astype(BF16)
                sb = jnp.where(mask_b, _dot_nt(qb[:, sl], kb[:, sl]), 0.0).astype(BF16)
                oa = jnp.dot(sa, vb[:half, sl], preferred_element_type=F32)
                ob = jnp.dot(sb, vb[:, sl], preferred_element_type=F32)
                o = o_inter + jnp.concatenate([oa, ob], axis=0)
                state_ref[h] = dl[:, sl] * st + _dot_tn(vb[:, sl], kd[:, sl])
                o_ref[rows, sl] = _hgrn_finish(o, gate_raw[:, sl], gw).astype(o_ref.dtype)

        @pl.when(jnp.logical_not(span <= HGRN_MAX_FACTORED_SPAN))
        def _():
            def head_body(h, hc):
                cols = pl.ds(pl.multiple_of(h * dk, dk), dk)
                o, new_state = _hgrn_exact_head(qs_ref[:, cols], ks_ref[:, cols], i_ref[rows, cols],
                                                gs_ref[:, cols], state_ref[h])
                state_ref[h] = new_state
                o_ref[rows, cols] = _hgrn_finish(o, g_ref[rows, cols], gw).astype(o_ref.dtype)
                return hc

            lax.fori_loop(0, HGRN_HEADS, head_body, 0)

        return carry

    lax.fori_loop(0, n_chunks, body, 0)


def hgrn_recurrence(proj, lb, gnorm_w, bsz, seq, block_len=256):
    n = bsz * seq
    nl = seq // block_len
    d = D_MODEL

    def col_spec(part):
        return pl.BlockSpec((block_len, d), lambda b, l: (b * nl + l, part))

    return pl.pallas_call(
        functools.partial(_hgrn_kernel, n_chunks=block_len // HGRN_CHUNK),
        grid=(bsz, nl),
        in_specs=[col_spec(0), col_spec(1), col_spec(2), col_spec(3),
                  pl.BlockSpec((1, d), lambda b, l: (0, 0)),
                  pl.BlockSpec((1, HGRN_DK), lambda b, l: (0, 0))],
        out_specs=pl.BlockSpec((block_len, d), lambda b, l: (b * nl + l, 0)),
        out_shape=jax.ShapeDtypeStruct((n, d), BF16),
        scratch_shapes=[pltpu.VMEM((HGRN_HEADS, HGRN_DK, HGRN_DK), F32),
                        pltpu.VMEM((HGRN_CHUNK, d), F32),
                        pltpu.VMEM((HGRN_CHUNK, d), F32),
                        pltpu.VMEM((HGRN_CHUNK, d), F32)],
        compiler_params=_cparams(("parallel", "arbitrary")),
        name="hgrn_recurrence",
    )(proj, proj, proj, proj, lb.reshape(1, d), gnorm_w.reshape(1, HGRN_DK))


def _sb_kernel(q_ref, k_ref, v_ref, o_ref, acc_ref, rem_ref, *, tq):
    i = pl.program_id(2)
    scale2 = (SB_HEAD_DIM ** -0.5) * LOG2E
    q = q_ref[...]
    lane = lax.broadcasted_iota(jnp.int32, (tq, 2 * SB_HEAD_DIM), 1)
    first_head = lane < SB_HEAD_DIM
    q_heads = (jnp.where(first_head, q, jnp.zeros_like(q)), jnp.where(first_head, jnp.zeros_like(q), q))
    diag_mask = (lax.broadcasted_iota(jnp.int32, (tq, tq), 1)
                 < lax.broadcasted_iota(jnp.int32, (tq, tq), 0))
    later = (lax.broadcasted_iota(jnp.int32, (tq, tq), 0)
             > lax.broadcasted_iota(jnp.int32, (tq, tq), 1)).astype(BF16)

    def sweep_block(kb, masked):
        k0 = pl.multiple_of(kb * tq, tq)
        kblk = k_ref[pl.ds(k0, tq), :]
        vblk = v_ref[pl.ds(k0, tq), :]
        rem_max = None
        for h in range(2):
            z = _dot_nt(q_heads[h], kblk) * scale2
            soft = jnp.log2(1.0 + jnp.exp2(-jnp.abs(z)))
            log_beta = jnp.minimum(z, 0.0) - soft
            log_rem = log_beta - z
            if masked:
                log_rem = jnp.where(diag_mask, log_rem, 0.0)
            between = jnp.dot(log_rem.astype(BF16), later, preferred_element_type=F32)
            if masked:
                w = jnp.exp2(jnp.where(diag_mask, log_beta + between, -jnp.inf))
                acc_ref[h] = jnp.dot(w.astype(BF16), vblk, preferred_element_type=F32)
                rem = jnp.sum(log_rem, axis=-1, keepdims=True)
            else:
                rem = rem_ref[h]
                w = jnp.exp2(log_beta + between + rem)
                acc_ref[h] += jnp.dot(w.astype(BF16), vblk, preferred_element_type=F32)
                rem = rem + jnp.sum(log_rem, axis=-1, keepdims=True)
            rem_ref[h] = rem
            m = jnp.max(rem)
            rem_max = m if rem_max is None else jnp.maximum(rem_max, m)
        return rem_max

    def cond(carry):
        kb, rem_max = carry
        return jnp.logical_and(kb >= 0, rem_max > EXP2_ZERO_BELOW)

    def body(carry):
        kb, _ = carry
        return kb - 1, sweep_block(kb, masked=False)

    lax.while_loop(cond, body, (i - 1, sweep_block(i, masked=True)))
    o_ref[...] = jnp.where(first_head, acc_ref[0], acc_ref[1]).astype(o_ref.dtype)


def stick_breaking(q, kv, bsz, seq, tq=256):
    n = bsz * seq
    nq = seq // tq
    pairs = SB_HEADS // 2
    w = 2 * SB_HEAD_DIM
    return pl.pallas_call(
        functools.partial(_sb_kernel, tq=tq),
        grid=(bsz, pairs, nq),
        in_specs=[
            pl.BlockSpec((tq, w), lambda b, p, i: (b * nq + i, p)),
            pl.BlockSpec((seq, w), lambda b, p, i: (b, p)),
            pl.BlockSpec((seq, w), lambda b, p, i: (b, pairs + p)),
        ],
        out_specs=pl.BlockSpec((tq, w), lambda b, p, i: (b * nq + i, p)),
        out_shape=jax.ShapeDtypeStruct((n, D_MODEL), BF16),
        scratch_shapes=[pltpu.VMEM((2, tq, w), F32), pltpu.VMEM((2, tq, 1), F32)],
        compiler_params=_cparams(("parallel", "parallel", "arbitrary")),
        name="stick_breaking",
    )(q, kv, kv)


def _swiglu_kernel(x_ref, nw_ref, wg_ref, wu_ref, wd_ref, o_ref, h_ref, acc_ref):
    j = pl.program_id(1)

    @pl.when(j == 0)
    def _():
        x = x_ref[...]
        h_ref[...] = _rmsnorm_rows(x, nw_ref[...]).astype(BF16)
        acc_ref[...] = x

    h = h_ref[...]
    gate = jnp.dot(h, wg_ref[...], preferred_element_type=F32)
    up = jnp.dot(h, wu_ref[...], preferred_element_type=F32)
    act = (gate * _sigmoid(gate) * up).astype(BF16)
    acc_ref[...] += jnp.dot(act, wd_ref[...], preferred_element_type=F32)

    @pl.when(j == pl.num_programs(1) - 1)
    def _():
        o_ref[...] = acc_ref[...]


def swiglu_residual(x, norm_w, w_gu_bf16, w_down_bf16, tm=512, tf=1408):
    n, d = x.shape
    f = w_down_bf16.shape[0]
    nf = f // tf
    return pl.pallas_call(
        _swiglu_kernel,
        grid=(n // tm, nf),
        in_specs=[
            pl.BlockSpec((tm, d), lambda i, j: (i, 0)),
            pl.BlockSpec((1, d), lambda i, j: (0, 0)),
            pl.BlockSpec((d, tf), lambda i, j: (0, j)),
            pl.BlockSpec((d, tf), lambda i, j: (0, nf + j)),
            pl.BlockSpec((tf, d), lambda i, j: (j, 0)),
        ],
        out_specs=pl.BlockSpec((tm, d), lambda i, j: (i, 0)),
        out_shape=jax.ShapeDtypeStruct((n, d), F32),
        scratch_shapes=[pltpu.VMEM((tm, d), BF16), pltpu.VMEM((tm, d), F32)],
        compiler_params=_cparams(("parallel", "arbitrary")),
        name="swiglu_residual",
    )(x, norm_w.reshape(1, d), w_gu_bf16, w_gu_bf16, w_down_bf16)


ROUTER_LANES = 128


def _router_kernel(x_ref, nw_ref, r_ref, l_ref):
    h = _rmsnorm_rows(x_ref[...], nw_ref[...])
    l_ref[...] = jnp.dot(h, r_ref[...], preferred_element_type=F32, precision=lax.Precision.HIGHEST)


def router_logits(x, norm_w, router, tm=1024):
    n, d = x.shape
    r_pad = jnp.zeros((d, ROUTER_LANES), F32).at[:, :N_EXPERTS].set(router)
    return pl.pallas_call(
        _router_kernel,
        grid=(n // tm,),
        in_specs=[
            pl.BlockSpec((tm, d), lambda i: (i, 0)),
            pl.BlockSpec((1, d), lambda i: (0, 0)),
            pl.BlockSpec((d, ROUTER_LANES), lambda i: (0, 0)),
        ],
        out_specs=pl.BlockSpec((tm, ROUTER_LANES), lambda i: (i, 0)),
        out_shape=jax.ShapeDtypeStruct((n, ROUTER_LANES), F32),
        compiler_params=_cparams(("parallel",)),
        name="moe_router",
    )(x, norm_w.reshape(1, d), r_pad)


def _moe_kernel(be_ref, nv_ref, nu_ref, asg_ref, asg_next_ref, x_hbm, nw_ref, wg_ref, wu_ref, wd_ref,
                out_hbm, xbuf, hbuf, acc_ref, ybuf, gsem, ssem, *, tm):
    i = pl.program_id(0)
    j = pl.program_id(1)
    last_j = pl.num_programs(1) - 1
    n_used = nu_ref[0]
    used = i < n_used

    def gather_copy(tok, slot, g, u):
        return pltpu.make_async_copy(x_hbm.at[pl.ds(tok, 1), :], xbuf.at[slot, g, pl.ds(u, 1), :], gsem.at[slot])

    def scatter_copy(dst, g, u):
        return pltpu.make_async_copy(ybuf.at[g, pl.ds(u, 1), :], out_hbm.at[pl.ds(dst, 1), :], ssem)

    def for_rows(count, issue):
        n_groups = lax.shift_right_logical(count, 3)

        def group(g, c):
            for u in range(8):
                issue(g * 8 + u, g, u)
            return c

        def single(r, c):
            issue(r, n_groups, r - n_groups * 8)
            return c

        lax.fori_loop(0, n_groups, group, 0)
        lax.fori_loop(n_groups * 8, count, single, 0)

    def start_gather(idx_ref, slot, count):
        for_rows(count, lambda r, g, u: gather_copy(lax.shift_right_logical(idx_ref[0, 0, r], 1), slot, g, u).start())

    def wait_rows(buf, sem, count):
        p = tm
        while p >= 1:
            part = buf.at[pl.ds(0, p // 8)] if p >= 8 else buf.at[0, pl.ds(0, p), :]

            @pl.when((count & p) != 0)
            def _(part=part):
                pltpu.make_async_copy(part, part, sem).wait()

            p //= 2

    @pl.when(jnp.logical_and(used, j == 0))
    def _():
        slot = lax.rem(i, 2)

        @pl.when(i == 0)
        def _():
            xbuf[...] = jnp.zeros_like(xbuf)
            start_gather(asg_ref, 0, nv_ref[0])

        wait_rows(xbuf.at[slot], gsem.at[slot], nv_ref[i])

        @pl.when(i + 1 < n_used)
        def _():
            start_gather(asg_next_ref, 1 - slot, nv_ref[i + 1])

        x = xbuf[slot].reshape(tm, xbuf.shape[-1])
        hbuf[...] = _rmsnorm_rows(x, nw_ref[...]).astype(BF16)

    @pl.when(used)
    def _():
        h = hbuf[...]
        gate = jnp.dot(h, wg_ref[0].astype(BF16), preferred_element_type=F32)
        up = jnp.dot(h, wu_ref[0].astype(BF16), preferred_element_type=F32)
        act = (gate * _sigmoid(gate) * up).astype(BF16)
        part = jnp.dot(act, wd_ref[0].astype(BF16), preferred_element_type=F32)

        @pl.when(j == 0)
        def _():
            acc_ref[...] = part

        @pl.when(jnp.logical_and(j > 0, j < last_j))
        def _():
            acc_ref[...] += part

        @pl.when(j == last_j)
        def _():
            @pl.when(i > 0)
            def _():
                wait_rows(ybuf, ssem, nv_ref[i - 1])

            ybuf[...] = (acc_ref[...] + part).reshape(ybuf.shape)

            for_rows(nv_ref[i], lambda r, g, u: scatter_copy(asg_ref[0, 0, r], g, u).start())

            @pl.when(i == n_used - 1)
            def _():
                wait_rows(ybuf, ssem, nv_ref[i])


def moe_experts(x, norm_w, asg, block_expert, n_valid, n_used, wg, wu, wd, tm, tf=512):
    n, d = x.shape
    f = wg.shape[2]
    n_blocks = asg.shape[0]
    nj = f // tf

    def j_eff(i, j, nu):
        return jnp.where(i < nu[0], j, nj - 1)

    grid_spec = pltpu.PrefetchScalarGridSpec(
        num_scalar_prefetch=3,
        grid=(n_blocks, nj),
        in_specs=[
            pl.BlockSpec((1, 1, tm), lambda i, j, be, nv, nu: (i, 0, 0), memory_space=pltpu.SMEM),
            pl.BlockSpec((1, 1, tm), lambda i, j, be, nv, nu: (jnp.minimum(i + 1, n_blocks - 1), 0, 0),
                         memory_space=pltpu.SMEM),
            pl.BlockSpec(memory_space=pl.ANY),
            pl.BlockSpec((1, d), lambda i, j, be, nv, nu: (0, 0)),
            pl.BlockSpec((1, d, tf), lambda i, j, be, nv, nu: (be[i], 0, j_eff(i, j, nu))),
            pl.BlockSpec((1, d, tf), lambda i, j, be, nv, nu: (be[i], 0, j_eff(i, j, nu))),
            pl.BlockSpec((1, tf, d), lambda i, j, be, nv, nu: (be[i], j_eff(i, j, nu), 0)),
        ],
        out_specs=pl.BlockSpec(memory_space=pl.ANY),
        scratch_shapes=[
            pltpu.VMEM((2, tm // 8, 8, d), F32),
            pltpu.VMEM((tm, d), BF16),
            pltpu.VMEM((tm, d), F32),
            pltpu.VMEM((tm // 8, 8, d), F32),
            pltpu.SemaphoreType.DMA((2,)),
            pltpu.SemaphoreType.DMA(()),
        ],
    )
    return pl.pallas_call(
        functools.partial(_moe_kernel, tm=tm),
        grid_spec=grid_spec,
        out_shape=jax.ShapeDtypeStruct((TOP_K * n, d), F32),
        compiler_params=_cparams(("arbitrary", "arbitrary")),
        name="moe_experts",
    )(block_expert, n_valid, n_used, asg, asg, x, norm_w.reshape(1, d), wg, wu, wd)


def _combine_kernel(x_ref, y_ref, g_ref, o_ref):
    d = x_ref.shape[1]
    g = g_ref[...]
    o_ref[...] = x_ref[...] + g[:, 0:1] * y_ref[:, :d] + g[:, 1:2] * y_ref[:, d:]


def _combine_norm_kernel(x_ref, y_ref, g_ref, w_ref, o_ref):
    d = x_ref.shape[1]
    g = g_ref[...]
    o_ref[...] = _rmsnorm_rows(x_ref[...] + g[:, 0:1] * y_ref[:, :d] + g[:, 1:2] * y_ref[:, d:], w_ref[...])


def moe_combine(x, y2, gates, out_norm_w=None, tm=512):
    n, d = x.shape
    in_specs = [pl.BlockSpec((tm, d), lambda i: (i, 0)),
                pl.BlockSpec((tm, TOP_K * d), lambda i: (i, 0)),
                pl.BlockSpec((tm, TOP_K), lambda i: (i, 0))]
    args = [x, y2, gates]
    body = _combine_kernel
    if out_norm_w is not None:
        in_specs.append(pl.BlockSpec((1, d), lambda i: (0, 0)))
        args.append(out_norm_w.reshape(1, d))
        body = _combine_norm_kernel
    return pl.pallas_call(
        body,
        grid=(n // tm,),
        in_specs=in_specs,
        out_specs=pl.BlockSpec((tm, d), lambda i: (i, 0)),
        out_shape=jax.ShapeDtypeStruct((n, d), F32),
        compiler_params=_cparams(("parallel",)),
        name="moe_combine",
    )(*args)


def moe_layer(x, norm_w, router, wg, wu, wd, out_norm_w=None, tm=1024):
    n, d = x.shape
    n_assign = n * TOP_K
    logits = router_logits(x, norm_w, router)
    top_logits, top_idx = lax.top_k(logits[:, :N_EXPERTS], TOP_K)
    gates = jax.nn.softmax(top_logits, axis=-1)

    flat_e = top_idx.reshape(-1).astype(jnp.int32)
    order = jnp.argsort(flat_e).astype(jnp.int32)
    counts = jnp.sum((flat_e[:, None] == jnp.arange(N_EXPERTS, dtype=jnp.int32)[None, :]).astype(jnp.int32), axis=0)
    starts = jnp.cumsum(counts) - counts
    n_blk_e = (counts + tm - 1) // tm
    blk_ends = jnp.cumsum(n_blk_e)
    blk_starts = blk_ends - n_blk_e
    n_blocks = n_assign // tm + N_EXPERTS
    blk = jnp.arange(n_blocks, dtype=jnp.int32)
    block_expert = jnp.minimum(jnp.searchsorted(blk_ends, blk, side="right"), N_EXPERTS - 1).astype(jnp.int32)
    n_used = blk_ends[-1].astype(jnp.int32).reshape(1)
    row0 = (blk - blk_starts[block_expert]) * tm
    n_valid = jnp.where(blk < n_used[0], jnp.clip(counts[block_expert] - row0, 0, tm), 0).astype(jnp.int32)
    pos = starts[block_expert][:, None] + row0[:, None] + jnp.arange(tm, dtype=jnp.int32)[None, :]
    asg = jnp.take(order, jnp.clip(pos, 0, n_assign - 1), axis=0).reshape(n_blocks, 1, tm)

    y = moe_experts(x, norm_w, asg, block_expert, n_valid, n_used, wg, wu, wd, tm)
    return moe_combine(x, y.reshape(n, TOP_K * d), gates, out_norm_w)


def kernel(x, mix_norm, ffn_norm, hgrn_w_in, hgrn_lb_raw, hgrn_gnorm, hgrn_w_out, kv_norm, w_kv, sb_w_q, sb_w_o, ffn_w_gu, ffn_w_down, moe_router, moe_w_gate, moe_w_up, moe_w_down, final_norm):
    bsz, seq, d = x.shape
    n = bsz * seq
    depth = mix_norm.shape[0]
    n_a = hgrn_w_in.shape[0]
    assert depth % 2 == 0, "the final rmsnorm is fused into the last (expert) layer's combine"
    xs = x.reshape(n, d)

    p = jax.nn.softmax(hgrn_lb_raw.astype(F32), axis=0)
    lower_bounds = jnp.cumsum(p, axis=0) - p[0:1]

    kv = None
    for layer in range(depth):
        if layer < n_a:
            proj = norm_matmul(xs, mix_norm[layer], hgrn_w_in[layer].astype(BF16), F32)
            og = hgrn_recurrence(proj, lower_bounds[layer], hgrn_gnorm[layer], bsz, seq)
            xs = matmul_residual(og, hgrn_w_out[layer].astype(BF16), xs)
        else:
            j = layer - n_a
            q = norm_matmul(xs, mix_norm[layer], sb_w_q[j].astype(BF16), BF16)
            att = stick_breaking(q, kv, bsz, seq)
            xs = matmul_residual(att, sb_w_o[j].astype(BF16), xs)
        if layer % 2 == 0:
            xs = swiglu_residual(xs, ffn_norm[layer], ffn_w_gu[layer // 2].astype(BF16),
                                 ffn_w_down[layer // 2].astype(BF16))
        else:
            e = layer // 2
            xs = moe_layer(xs, ffn_norm[layer], moe_router[e], moe_w_gate[e], moe_w_up[e], moe_w_down[e],
                           out_norm_w=final_norm if layer == depth - 1 else None)
        if layer == n_a - 1:
            kv = norm_matmul(xs, kv_norm, w_kv.astype(BF16), BF16)
    return xs.reshape(bsz, seq, d)
```

```python
import functools

import jax
import jax.numpy as jnp
from jax import lax
from jax.experimental import pallas as pl
from jax.experimental.pallas import tpu as pltpu

F32 = jnp.float32
BF16 = jnp.bfloat16

D_MODEL = 1024
HGRN_HEADS = 8
HGRN_DK = 128
HGRN_CHUNK = 64
SB_HEADS = 16
SB_HEAD_DIM = 64
N_EXPERTS = 8
TOP_K = 2
EPS = 1e-6
F_FLOOR = 1e-30
LOG2E = 1.4426950408889634

VMEM_LIMIT_BYTES = 52 * 1024 * 1024
EXP2_ZERO_BELOW = -150.0
HGRN_MAX_FACTORED_SPAN = 100.0
SB_LOGIT_SCALE_LOG2 = (SB_HEAD_DIM ** -0.5) * LOG2E


def _cparams(sem):
    return pltpu.CompilerParams(dimension_semantics=sem, vmem_limit_bytes=VMEM_LIMIT_BYTES)


def _sigmoid(x):
    return 0.5 * jnp.tanh(0.5 * x) + 0.5


def _rmsnorm_rows(x, w):
    ms = jnp.mean(x * x, axis=-1, keepdims=True)
    return x * lax.rsqrt(ms + EPS) * w


def _dot_nt(a, b):
    return lax.dot_general(a, b, (((1,), (1,)), ((), ())), preferred_element_type=F32)


def _dot_tn(a, b):
    return lax.dot_general(a, b, (((0,), (0,)), ((), ())), preferred_element_type=F32)


def _norm_matmul_kernel(x_ref, nw_ref, w_ref, o_ref, h_ref, *, out_scale):
    @pl.when(pl.program_id(1) == 0)
    def _():
        h_ref[...] = _rmsnorm_rows(x_ref[...], nw_ref[...]).astype(BF16)

    acc = jnp.dot(h_ref[...], w_ref[...], preferred_element_type=F32)
    if out_scale is not None:
        acc = acc * out_scale
    o_ref[...] = acc.astype(o_ref.dtype)


def norm_matmul(x, norm_w, w_bf16, out_dtype, out_scale=None, tm=1024, tn=1024):
    n, d = x.shape
    n_out = w_bf16.shape[1]
    return pl.pallas_call(
        functools.partial(_norm_matmul_kernel, out_scale=out_scale),
        grid=(n // tm, n_out // tn),
        in_specs=[
            pl.BlockSpec((tm, d), lambda i, j: (i, 0)),
            pl.BlockSpec((1, d), lambda i, j: (0, 0)),
            pl.BlockSpec((d, tn), lambda i, j: (0, j)),
        ],
        out_specs=pl.BlockSpec((tm, tn), lambda i, j: (i, j)),
        out_shape=jax.ShapeDtypeStruct((n, n_out), out_dtype),
        scratch_shapes=[pltpu.VMEM((tm, d), BF16)],
        compiler_params=_cparams(("parallel", "arbitrary")),
        name="norm_matmul",
    )(x, norm_w.reshape(1, d), w_bf16)


def _matmul_res_kernel(a_ref, w_ref, r_ref, o_ref):
    o_ref[...] = r_ref[...] + jnp.dot(a_ref[...], w_ref[...], preferred_element_type=F32)


def matmul_residual(a_bf16, w_bf16, res, tm=1024, tn=1024):
    n, k = a_bf16.shape
    n_out = w_bf16.shape[1]
    return pl.pallas_call(
        _matmul_res_kernel,
        grid=(n // tm, n_out // tn),
        in_specs=[
            pl.BlockSpec((tm, k), lambda i, j: (i, 0)),
            pl.BlockSpec((k, tn), lambda i, j: (0, j)),
            pl.BlockSpec((tm, tn), lambda i, j: (i, j)),
        ],
        out_specs=pl.BlockSpec((tm, tn), lambda i, j: (i, j)),
        out_shape=jax.ShapeDtypeStruct((n, n_out), F32),
        compiler_params=_cparams(("parallel", "parallel")),
        name="matmul_residual",
    )(a_bf16, w_bf16, res)


def _cumsum_rows(x, row):
    n = x.shape[0]
    s = 1
    while s < n:
        x = x + jnp.where(row >= s, pltpu.roll(x, s, axis=0), 0.0)
        s *= 2
    return x


def _hgrn_finish(o, gate_raw, gw):
    o = o * lax.rsqrt(jnp.mean(o * o, axis=-1, keepdims=True) + EPS) * gw
    return o * (gate_raw * _sigmoid(gate_raw))


def _hgrn_exact_head(q, k, v, g, state_t):
    c = HGRN_CHUNK
    qg = (q * jnp.exp2(g)).astype(BF16)
    o_inter = _dot_nt(qg, state_t.astype(BF16))
    nv = c // 8
    g8 = [g[8 * j:8 * (j + 1), :] for j in range(nv)]
    q8 = [q[8 * j:8 * (j + 1), :] for j in range(nv)]
    acc8 = [o_inter[8 * j:8 * (j + 1), :] for j in range(nv)]
    row8 = lax.broadcasted_iota(jnp.int32, (8, HGRN_DK), 0)
    for s in range(c):
        j0, r = divmod(s, 8)
        gs = g8[j0][r:r + 1, :]
        ks = k[s:s + 1, :]
        vs = v[s:s + 1, :]
        for j in range(j0, nv):
            diff = g8[j] - gs
            if j == j0 and r > 0:
                m = row8 >= r
                p = jnp.where(m, q8[j] * jnp.exp2(jnp.where(m, diff, 0.0)) * ks, 0.0)
            else:
                p = q8[j] * jnp.exp2(diff) * ks
            cs = jnp.sum(p, axis=-1, keepdims=True)
            acc8[j] = acc8[j] + cs * vs
    o = jnp.concatenate(acc8, axis=0)
    g_last = g[c - 1:c, :]
    kd = (k * jnp.exp2(g_last - g)).astype(BF16)
    new_state_t = jnp.exp2(g_last) * state_t + _dot_tn(v.astype(BF16), kd)
    return o, new_state_t


def _hgrn_kernel(q_ref, f_ref, i_ref, g_ref, lb_ref, gw_ref, o_ref, state_ref, qs_ref, ks_ref, gs_ref,
                 *, n_chunks):
    @pl.when(pl.program_id(1) == 0)
    def _():
        state_ref[...] = jnp.zeros_like(state_ref)

    c = HGRN_CHUNK
    half = c // 2
    dk = HGRN_DK
    lb = lb_ref[...]
    gw = gw_ref[...]
    row = lax.broadcasted_iota(jnp.int32, (c, D_MODEL), 0)
    t_idx = lax.broadcasted_iota(jnp.int32, (c, 2 * c), 0)
    col = lax.broadcasted_iota(jnp.int32, (c, 2 * c), 1)
    upper = jnp.where(t_idx >= half, t_idx, t_idx + c)
    lower = jnp.where(t_idx >= half, 0, c)
    score_mask = jnp.logical_and(col >= lower, col <= upper)
    pad_rows = jnp.zeros((half, D_MODEL), BF16)

    def body(ci, carry):
        rows = pl.ds(pl.multiple_of(ci * c, c), c)
        qr = q_ref[rows, :]
        fr = f_ref[rows, :]
        q = qr * _sigmoid(qr)
        sig = _sigmoid(fr)
        f = lb + (1.0 - lb) * sig
        k = (1.0 - lb) * (1.0 - sig)
        g = _cumsum_rows(jnp.log2(jnp.maximum(f, F_FLOOR)), row)
        qs_ref[...] = q
        ks_ref[...] = k
        gs_ref[...] = g
        g_mid = g[half - 1:half, :]
        g_last = g[c - 1:c, :]
        span = jnp.maximum(jnp.max(-g_mid), jnp.max(g_mid - g_last))

        @pl.when(span <= HGRN_MAX_FACTORED_SPAN)
        def _():
            q = qs_ref[...]
            k = ks_ref[...]
            g = gs_ref[...]
            qg = (q * jnp.exp2(g)).astype(BF16)
            qb = (q[half:, :] * jnp.exp2(g[half:, :] - g_mid)).astype(BF16)
            kb_f32 = k * jnp.exp2(g_mid - g)
            ka = (kb_f32[:half, :] * jnp.exp2(-g_mid)).astype(BF16)
            kd = (kb_f32 * jnp.exp2(g_last - g_mid)).astype(BF16)
            dl = jnp.exp2(g_last)
            vb = i_ref[rows, :].astype(BF16)
            gate_raw = g_ref[rows, :]
            q_mix = jnp.concatenate([qg[:half, :], qb], axis=0)
            k_stack = jnp.concatenate([kb_f32.astype(BF16), ka, pad_rows], axis=0)
            v_stack = jnp.concatenate([vb, vb[:half, :], pad_rows], axis=0)
            heads = [slice(h * dk, (h + 1) * dk) for h in range(HGRN_HEADS)]
            states = [state_ref[h] for h in range(HGRN_HEADS)]
            scores = [_dot_nt(q_mix[:, sl], k_stack[:, sl]) for sl in heads]
            inter = [_dot_nt(qg[:, sl], st.astype(BF16)) for sl, st in zip(heads, states)]
            upd = [_dot_tn(vb[:, sl], kd[:, sl]) for sl in heads]
            scores = [jnp.where(score_mask, s, 0.0).astype(BF16) for s in scores]
            outs = [oi + jnp.dot(s, v_stack[:, sl], preferred_element_type=F32)
                    for oi, s, sl in zip(inter, scores, heads)]
            for h, sl in enumerate(heads):
                state_ref[h] = dl[:, sl] * states[h] + upd[h]
                o_ref[rows, sl] = _hgrn_finish(outs[h], gate_raw[:, sl], gw).astype(o_ref.dtype)

        @pl.when(jnp.logical_not(span <= HGRN_MAX_FACTORED_SPAN))
        def _():
            def head_body(h, hc):
                cols = pl.ds(pl.multiple_of(h * dk, dk), dk)
                o, new_state = _hgrn_exact_head(qs_ref[:, cols], ks_ref[:, cols], i_ref[rows, cols],
                                                gs_ref[:, cols], state_ref[h])
                state_ref[h] = new_state
                o_ref[rows, cols] = _hgrn_finish(o, g_ref[rows, cols], gw).astype(o_ref.dtype)
                return hc

            lax.fori_loop(0, HGRN_HEADS, head_body, 0)

        return carry

    lax.fori_loop(0, n_chunks, body, 0)


def hgrn_recurrence(proj, lb, gnorm_w, bsz, seq, block_len=256):
    n = bsz * seq
    nl = seq // block_len
    d = D_MODEL

    def col_spec(part):
        return pl.BlockSpec((block_len, d), lambda b, l: (b * nl + l, part))

    return pl.pallas_call(
        functools.partial(_hgrn_kernel, n_chunks=block_len // HGRN_CHUNK),
        grid=(bsz, nl),
        in_specs=[col_spec(0), col_spec(1), col_spec(2), col_spec(3),
                  pl.BlockSpec((1, d), lambda b, l: (0, 0)),
                  pl.BlockSpec((1, HGRN_DK), lambda b, l: (0, 0))],
        out_specs=pl.BlockSpec((block_len, d), lambda b, l: (b * nl + l, 0)),
        out_shape=jax.ShapeDtypeStruct((n, d), BF16),
        scratch_shapes=[pltpu.VMEM((HGRN_HEADS, HGRN_DK, HGRN_DK), F32),
                        pltpu.VMEM((HGRN_CHUNK, d), F32),
                        pltpu.VMEM((HGRN_CHUNK, d), F32),
                        pltpu.VMEM((HGRN_CHUNK, d), F32)],
        compiler_params=_cparams(("parallel", "arbitrary")),
        name="hgrn_recurrence",
    )(proj, proj, proj, proj, lb.reshape(1, d), gnorm_w.reshape(1, HGRN_DK))


def _sb_kernel(q_ref, k_ref, v_ref, o_ref, acc_ref, rem_ref, *, tq, n_pairs):
    i = pl.program_id(2)
    pw = 2 * SB_HEAD_DIM
    n_heads = 2 * n_pairs
    first_head = lax.broadcasted_iota(jnp.int32, (tq, pw), 1) < SB_HEAD_DIM
    q_pairs = []
    for p in range(n_pairs):
        q = q_ref[:, p * pw:(p + 1) * pw]
        zero = jnp.zeros_like(q)
        q_pairs.append(jnp.concatenate([jnp.where(first_head, q, zero), jnp.where(first_head, zero, q)], axis=0))
    diag_mask = (lax.broadcasted_iota(jnp.int32, (n_heads * tq, tq), 1)
                 < lax.broadcasted_iota(jnp.int32, (n_heads * tq, tq), 0) % tq)
    later = (lax.broadcasted_iota(jnp.int32, (tq, tq), 0)
             > lax.broadcasted_iota(jnp.int32, (tq, tq), 1)).astype(BF16)

    def sweep_block(kb, masked):
        k0 = pl.multiple_of(kb * tq, tq)
        z = jnp.concatenate(
            [_dot_nt(q_pairs[p], k_ref[pl.ds(k0, tq), p * pw:(p + 1) * pw]) for p in range(n_pairs)],
            axis=0)
        neg_abs = pltpu.bitcast(pltpu.bitcast(z, jnp.uint32) | jnp.uint32(0x80000000), F32)
        soft = jnp.log2(1.0 + jnp.exp2(neg_abs))
        log_beta = jnp.minimum(z, 0.0) - soft
        log_rem = log_beta - z
        if masked:
            log_rem = jnp.where(diag_mask, log_rem, 0.0)
        between = jnp.dot(log_rem.astype(BF16), later, preferred_element_type=F32)
        if masked:
            w = jnp.exp2(jnp.where(diag_mask, log_beta + between, -jnp.inf)).astype(BF16)
            rem = jnp.sum(log_rem, axis=-1, keepdims=True)
        else:
            rem = rem_ref[...]
            w = jnp.exp2(log_beta + between + rem).astype(BF16)
            rem = rem + jnp.sum(log_rem, axis=-1, keepdims=True)
        rem_ref[...] = rem
        for p in range(n_pairs):
            rows = slice(2 * p * tq, 2 * (p + 1) * tq)
            pv = jnp.dot(w[rows, :], v_ref[pl.ds(k0, tq), p * pw:(p + 1) * pw], preferred_element_type=F32)
            if masked:
                acc_ref[rows, :] = pv
            else:
                acc_ref[rows, :] += pv
        return jnp.max(rem)

    def cond(carry):
        kb, rem_max = carry
        return jnp.logical_and(kb >= 0, rem_max > EXP2_ZERO_BELOW)

    def body(carry):
        kb, _ = carry
        return kb - 1, sweep_block(kb, masked=False)

    lax.while_loop(cond, body, (i - 1, sweep_block(i, masked=True)))
    for p in range(n_pairs):
        first = acc_ref[2 * p * tq:(2 * p + 1) * tq, :]
        second = acc_ref[(2 * p + 1) * tq:(2 * p + 2) * tq, :]
        o_ref[:, p * pw:(p + 1) * pw] = jnp.where(first_head, first, second).astype(o_ref.dtype)


def stick_breaking(q, kv, bsz, seq, tq=256, n_pairs=2):
    n = bsz * seq
    nq = seq // tq
    groups = SB_HEADS // (2 * n_pairs)
    w = 2 * SB_HEAD_DIM * n_pairs
    return pl.pallas_call(
        functools.partial(_sb_kernel, tq=tq, n_pairs=n_pairs),
        grid=(bsz, groups, nq),
        in_specs=[
            pl.BlockSpec((tq, w), lambda b, p, i: (b * nq + i, p)),
            pl.BlockSpec((seq, w), lambda b, p, i: (b, p)),
            pl.BlockSpec((seq, w), lambda b, p, i: (b, groups + p)),
        ],
        out_specs=pl.BlockSpec((tq, w), lambda b, p, i: (b * nq + i, p)),
        out_shape=jax.ShapeDtypeStruct((n, D_MODEL), BF16),
        scratch_shapes=[pltpu.VMEM((2 * n_pairs * tq, 2 * SB_HEAD_DIM), F32),
                        pltpu.VMEM((2 * n_pairs * tq, 1), F32)],
        compiler_params=_cparams(("parallel", "parallel", "arbitrary")),
        name="stick_breaking",
    )(q, kv, kv)


def _swiglu_kernel(x_ref, nw_ref, wg_ref, wu_ref, wd_ref, o_ref, h_ref, acc_ref):
    j = pl.program_id(1)

    @pl.when(j == 0)
    def _():
        x = x_ref[...]
        h_ref[...] = _rmsnorm_rows(x, nw_ref[...]).astype(BF16)
        acc_ref[...] = x

    h = h_ref[...]
    gate = jnp.dot(h, wg_ref[...], preferred_element_type=F32)
    up = jnp.dot(h, wu_ref[...], preferred_element_type=F32)
    act = (gate * _sigmoid(gate) * up).astype(BF16)
    acc_ref[...] += jnp.dot(act, wd_ref[...], preferred_element_type=F32)

    @pl.when(j == pl.num_programs(1) - 1)
    def _():
        o_ref[...] = acc_ref[...]


def swiglu_residual(x, norm_w, w_gu_bf16, w_down_bf16, tm=512, tf=1408):
    n, d = x.shape
    f = w_down_bf16.shape[0]
    nf = f // tf
    return pl.pallas_call(
        _swiglu_kernel,
        grid=(n // tm, nf),
        in_specs=[
            pl.BlockSpec((tm, d), lambda i, j: (i, 0)),
            pl.BlockSpec((1, d), lambda i, j: (0, 0)),
            pl.BlockSpec((d, tf), lambda i, j: (0, j)),
            pl.BlockSpec((d, tf), lambda i, j: (0, nf + j)),
            pl.BlockSpec((tf, d), lambda i, j: (j, 0)),
        ],
        out_specs=pl.BlockSpec((tm, d), lambda i, j: (i, 0)),
        out_shape=jax.ShapeDtypeStruct((n, d), F32),
        scratch_shapes=[pltpu.VMEM((tm, d), BF16), pltpu.VMEM((tm, d), F32)],
        compiler_params=_cparams(("parallel", "arbitrary")),
        name="swiglu_residual",
    )(x, norm_w.reshape(1, d), w_gu_bf16, w_gu_bf16, w_down_bf16)


ROUTER_LANES = 128


def _router_kernel(x_ref, nw_ref, r_ref, l_ref):
    h = _rmsnorm_rows(x_ref[...], nw_ref[...])
    l_ref[...] = jnp.dot(h, r_ref[...], preferred_element_type=F32, precision=lax.Precision.HIGHEST)


def router_logits(x, norm_w, router, tm=1024):
    n, d = x.shape
    r_pad = jnp.zeros((d, ROUTER_LANES), F32).at[:, :N_EXPERTS].set(router)
    return pl.pallas_call(
        _router_kernel,
        grid=(n // tm,),
        in_specs=[
            pl.BlockSpec((tm, d), lambda i: (i, 0)),
            pl.BlockSpec((1, d), lambda i: (0, 0)),
            pl.BlockSpec((d, ROUTER_LANES), lambda i: (0, 0)),
        ],
        out_specs=pl.BlockSpec((tm, ROUTER_LANES), lambda i: (i, 0)),
        out_shape=jax.ShapeDtypeStruct((n, ROUTER_LANES), F32),
        compiler_params=_cparams(("parallel",)),
        name="moe_router",
    )(x, norm_w.reshape(1, d), r_pad)


def _moe_kernel(be_ref, nv_ref, nu_ref, asg_ref, asg_next_ref, x_hbm, nw_ref, wg_ref, wu_ref, wd_ref,
                out_hbm, xbuf, hbuf, acc_ref, ybuf, gsem, ssem, *, tm):
    i = pl.program_id(0)
    j = pl.program_id(1)
    last_j = pl.num_programs(1) - 1
    n_used = nu_ref[0]
    used = i < n_used

    def gather_copy(tok, slot, g, u):
        return pltpu.make_async_copy(x_hbm.at[pl.ds(tok, 1), :], xbuf.at[slot, g, pl.ds(u, 1), :], gsem.at[slot])

    def scatter_copy(a, g, u):
        return pltpu.make_async_copy(ybuf.at[g, pl.ds(u, 1), :],
                                     out_hbm.at[a & 1, pl.ds(lax.shift_right_logical(a, 1), 1), :], ssem)

    def for_rows(count, issue):
        n_groups = lax.shift_right_logical(count, 3)

        def group(g, c):
            for u in range(8):
                issue(g * 8 + u, g, u)
            return c

        def single(r, c):
            issue(r, n_groups, r - n_groups * 8)
            return c

        lax.fori_loop(0, n_groups, group, 0)
        lax.fori_loop(n_groups * 8, count, single, 0)

    def start_gather(idx_ref, slot, count):
        for_rows(count, lambda r, g, u: gather_copy(lax.shift_right_logical(idx_ref[0, 0, r], 1), slot, g, u).start())

    def wait_rows(buf, sem, count):
        p = tm
        while p >= 1:
            part = buf.at[pl.ds(0, p // 8)] if p >= 8 else buf.at[0, pl.ds(0, p), :]

            @pl.when((count & p) != 0)
            def _(part=part):
                pltpu.make_async_copy(part, part, sem).wait()

            p //= 2

    @pl.when(jnp.logical_and(used, j == 0))
    def _():
        slot = lax.rem(i, 2)

        @pl.when(i == 0)
        def _():
            xbuf[...] = jnp.zeros_like(xbuf)
            start_gather(asg_ref, 0, nv_ref[0])

        wait_rows(xbuf.at[slot], gsem.at[slot], nv_ref[i])

        @pl.when(i + 1 < n_used)
        def _():
            start_gather(asg_next_ref, 1 - slot, nv_ref[i + 1])

        x = xbuf[slot].reshape(tm, xbuf.shape[-1])
        hbuf[...] = _rmsnorm_rows(x, nw_ref[...]).astype(BF16)

    @pl.when(used)
    def _():
        h = hbuf[...]
        gate = jnp.dot(h, wg_ref[0, 0].astype(BF16), preferred_element_type=F32)
        up = jnp.dot(h, wu_ref[0, 0].astype(BF16), preferred_element_type=F32)
        act = (gate * _sigmoid(gate) * up).astype(BF16)
        part = jnp.dot(act, wd_ref[0, 0].astype(BF16), preferred_element_type=F32)

        @pl.when(j == 0)
        def _():
            acc_ref[...] = part

        @pl.when(jnp.logical_and(j > 0, j < last_j))
        def _():
            acc_ref[...] += part

        @pl.when(j == last_j)
        def _():
            @pl.when(i > 0)
            def _():
                wait_rows(ybuf, ssem, nv_ref[i - 1])

            ybuf[...] = (acc_ref[...] + part).reshape(ybuf.shape)

            for_rows(nv_ref[i], lambda r, g, u: scatter_copy(asg_ref[0, 0, r], g, u).start())

            @pl.when(i == n_used - 1)
            def _():
                wait_rows(ybuf, ssem, nv_ref[i])


def moe_experts(x, norm_w, asg, block_expert, n_valid, n_used, wg, wu, wd, moe_idx, tm, tf=512):
    n, d = x.shape
    f = wg.shape[3]
    n_blocks = asg.shape[0]
    nj = f // tf

    def j_eff(i, j, nu):
        return jnp.where(i < nu[0], j, nj - 1)

    grid_spec = pltpu.PrefetchScalarGridSpec(
        num_scalar_prefetch=3,
        grid=(n_blocks, nj),
        in_specs=[
            pl.BlockSpec((1, 1, tm), lambda i, j, be, nv, nu: (i, 0, 0), memory_space=pltpu.SMEM),
            pl.BlockSpec((1, 1, tm), lambda i, j, be, nv, nu: (jnp.minimum(i + 1, n_blocks - 1), 0, 0),
                         memory_space=pltpu.SMEM),
            pl.BlockSpec(memory_space=pl.ANY),
            pl.BlockSpec((1, d), lambda i, j, be, nv, nu: (0, 0)),
            pl.BlockSpec((1, 1, d, tf), lambda i, j, be, nv, nu: (moe_idx, be[i], 0, j_eff(i, j, nu))),
            pl.BlockSpec((1, 1, d, tf), lambda i, j, be, nv, nu: (moe_idx, be[i], 0, j_eff(i, j, nu))),
            pl.BlockSpec((1, 1, tf, d), lambda i, j, be, nv, nu: (moe_idx, be[i], j_eff(i, j, nu), 0)),
        ],
        out_specs=pl.BlockSpec(memory_space=pl.ANY),
        scratch_shapes=[
            pltpu.VMEM((2, tm // 8, 8, d), F32),
            pltpu.VMEM((tm, d), BF16),
            pltpu.VMEM((tm, d), F32),
            pltpu.VMEM((tm // 8, 8, d), F32),
            pltpu.SemaphoreType.DMA((2,)),
            pltpu.SemaphoreType.DMA(()),
        ],
    )
    return pl.pallas_call(
        functools.partial(_moe_kernel, tm=tm),
        grid_spec=grid_spec,
        out_shape=jax.ShapeDtypeStruct((TOP_K, n, d), F32),
        compiler_params=_cparams(("arbitrary", "arbitrary")),
        name="moe_experts",
    )(block_expert, n_valid, n_used, asg, asg, x, norm_w.reshape(1, d), wg, wu, wd)


def _combine_kernel(x_ref, y0_ref, y1_ref, g_ref, o_ref):
    g = g_ref[...]
    o_ref[...] = x_ref[...] + g[:, 0:1] * y0_ref[0] + g[:, 1:2] * y1_ref[0]


def _combine_norm_kernel(x_ref, y0_ref, y1_ref, g_ref, w_ref, o_ref):
    g = g_ref[...]
    o_ref[...] = _rmsnorm_rows(x_ref[...] + g[:, 0:1] * y0_ref[0] + g[:, 1:2] * y1_ref[0], w_ref[...])


def moe_combine(x, y, gates, out_norm_w=None, tm=512):
    n, d = x.shape
    in_specs = [pl.BlockSpec((tm, d), lambda i: (i, 0)),
                pl.BlockSpec((1, tm, d), lambda i: (0, i, 0)),
                pl.BlockSpec((1, tm, d), lambda i: (1, i, 0)),
                pl.BlockSpec((tm, TOP_K), lambda i: (i, 0))]
    args = [x, y, y, gates]
    body = _combine_kernel
    if out_norm_w is not None:
        in_specs.append(pl.BlockSpec((1, d), lambda i: (0, 0)))
        args.append(out_norm_w.reshape(1, d))
        body = _combine_norm_kernel
    return pl.pallas_call(
        body,
        grid=(n // tm,),
        in_specs=in_specs,
        out_specs=pl.BlockSpec((tm, d), lambda i: (i, 0)),
        out_shape=jax.ShapeDtypeStruct((n, d), F32),
        compiler_params=_cparams(("parallel",)),
        name="moe_combine",
    )(*args)


def moe_layer(x, norm_w, router, wg, wu, wd, moe_idx, out_norm_w=None, tm=1024):
    n, d = x.shape
    n_assign = n * TOP_K
    logits = router_logits(x, norm_w, router)
    top_logits, top_idx = lax.top_k(logits[:, :N_EXPERTS], TOP_K)
    gates = jax.nn.softmax(top_logits, axis=-1)

    flat_e = top_idx.reshape(-1).astype(jnp.int32)
    order = jnp.argsort(flat_e).astype(jnp.int32)
    counts = jnp.sum((flat_e[:, None] == jnp.arange(N_EXPERTS, dtype=jnp.int32)[None, :]).astype(jnp.int32), axis=0)
    starts = jnp.cumsum(counts) - counts
    n_blk_e = (counts + tm - 1) // tm
    blk_ends = jnp.cumsum(n_blk_e)
    blk_starts = blk_ends - n_blk_e
    n_blocks = n_assign // tm + N_EXPERTS
    blk = jnp.arange(n_blocks, dtype=jnp.int32)
    block_expert = jnp.minimum(jnp.searchsorted(blk_ends, blk, side="right"), N_EXPERTS - 1).astype(jnp.int32)
    n_used = blk_ends[-1].astype(jnp.int32).reshape(1)
    row0 = (blk - blk_starts[block_expert]) * tm
    n_valid = jnp.where(blk < n_used[0], jnp.clip(counts[block_expert] - row0, 0, tm), 0).astype(jnp.int32)
    pos = starts[block_expert][:, None] + row0[:, None] + jnp.arange(tm, dtype=jnp.int32)[None, :]
    asg = jnp.take(order, jnp.clip(pos, 0, n_assign - 1), axis=0).reshape(n_blocks, 1, tm)

    y = moe_experts(x, norm_w, asg, block_expert, n_valid, n_used, wg, wu, wd, moe_idx, tm)
    return moe_combine(x, y, gates, out_norm_w)


def kernel(x, mix_norm, ffn_norm, hgrn_w_in, hgrn_lb_raw, hgrn_gnorm, hgrn_w_out, kv_norm, w_kv, sb_w_q, sb_w_o, ffn_w_gu, ffn_w_down, moe_router, moe_w_gate, moe_w_up, moe_w_down, final_norm):
    bsz, seq, d = x.shape
    n = bsz * seq
    depth = mix_norm.shape[0]
    n_a = hgrn_w_in.shape[0]
    assert depth % 2 == 0, "the final rmsnorm is fused into the last (expert) layer's combine"
    xs = x.reshape(n, d)

    p = jax.nn.softmax(hgrn_lb_raw.astype(F32), axis=0)
    lower_bounds = jnp.cumsum(p, axis=0) - p[0:1]

    kv = None
    for layer in range(depth):
        if layer < n_a:
            proj = norm_matmul(xs, mix_norm[layer], hgrn_w_in[layer].astype(BF16), F32)
            og = hgrn_recurrence(proj, lower_bounds[layer], hgrn_gnorm[layer], bsz, seq)
            xs = matmul_residual(og, hgrn_w_out[layer].astype(BF16), xs)
        else:
            j = layer - n_a
            q = norm_matmul(xs, mix_norm[layer], sb_w_q[j].astype(BF16), BF16, out_scale=SB_LOGIT_SCALE_LOG2)
            att = stick_breaking(q, kv, bsz, seq)
            xs = matmul_residual(att, sb_w_o[j].astype(BF16), xs)
        if layer % 2 == 0:
            xs = swiglu_residual(xs, ffn_norm[layer], ffn_w_gu[layer // 2].astype(BF16),
                                 ffn_w_down[layer // 2].astype(BF16))
        else:
            e = layer // 2
            xs = moe_layer(xs, ffn_norm[layer], moe_router[e], moe_w_gate, moe_w_up, moe_w_down, e,
                           out_norm_w=final_norm if layer == depth - 1 else None)
        if layer == n_a - 1:
            kv = norm_matmul(xs, kv_norm, w_kv.astype(BF16), BF16)
    return xs.reshape(bsz, seq, d)
```

```python
import functools

import jax
import jax.numpy as jnp
from jax import lax
from jax.experimental import pallas as pl
from jax.experimental.pallas import tpu as pltpu

F32 = jnp.float32
BF16 = jnp.bfloat16

D_MODEL = 1024
HGRN_HEADS = 8
HGRN_DK = 128
HGRN_CHUNK = 64
SB_HEADS = 16
SB_HEAD_DIM = 64
N_EXPERTS = 8
TOP_K = 2
EPS = 1e-6
F_FLOOR = 1e-30
LOG2E = 1.4426950408889634

VMEM_LIMIT_BYTES = 52 * 1024 * 1024
EXP2_ZERO_BELOW = -150.0
HGRN_MAX_FACTORED_SPAN = 100.0
SB_LOGIT_SCALE_LOG2 = (SB_HEAD_DIM ** -0.5) * LOG2E


def _cparams(sem):
    return pltpu.CompilerParams(dimension_semantics=sem, vmem_limit_bytes=VMEM_LIMIT_BYTES)


def _sigmoid(x):
    return 0.5 * jnp.tanh(0.5 * x) + 0.5


def _rmsnorm_rows(x, w):
    ms = jnp.mean(x * x, axis=-1, keepdims=True)
    return x * lax.rsqrt(ms + EPS) * w


def _dot_nt(a, b):
    return lax.dot_general(a, b, (((1,), (1,)), ((), ())), preferred_element_type=F32)


def _dot_tn(a, b):
    return lax.dot_general(a, b, (((0,), (0,)), ((), ())), preferred_element_type=F32)


def _norm_matmul_kernel(x_ref, nw_ref, w_ref, o_ref, h_ref, *, out_scale):
    @pl.when(pl.program_id(1) == 0)
    def _():
        h_ref[...] = _rmsnorm_rows(x_ref[...], nw_ref[...]).astype(BF16)

    acc = jnp.dot(h_ref[...], w_ref[0].astype(BF16), preferred_element_type=F32)
    if out_scale is not None:
        acc = acc * out_scale
    o_ref[...] = acc.astype(o_ref.dtype)


def norm_matmul(x, norm_w, w_stack, layer, out_dtype, out_scale=None, tm=1024, tn=1024):
    n, d = x.shape
    n_out = w_stack.shape[2]
    return pl.pallas_call(
        functools.partial(_norm_matmul_kernel, out_scale=out_scale),
        grid=(n // tm, n_out // tn),
        in_specs=[
            pl.BlockSpec((tm, d), lambda i, j: (i, 0)),
            pl.BlockSpec((1, d), lambda i, j: (0, 0)),
            pl.BlockSpec((1, d, tn), lambda i, j: (layer, 0, j)),
        ],
        out_specs=pl.BlockSpec((tm, tn), lambda i, j: (i, j)),
        out_shape=jax.ShapeDtypeStruct((n, n_out), out_dtype),
        scratch_shapes=[pltpu.VMEM((tm, d), BF16)],
        compiler_params=_cparams(("parallel", "arbitrary")),
        name="norm_matmul",
    )(x, norm_w.reshape(1, d), w_stack)


def _matmul_res_kernel(a_ref, w_ref, r_ref, o_ref):
    o_ref[...] = r_ref[...] + jnp.dot(a_ref[...], w_ref[0].astype(BF16), preferred_element_type=F32)


def matmul_residual(a_bf16, w_stack, layer, res, tm=1024):
    n, k = a_bf16.shape
    n_out = w_stack.shape[2]
    return pl.pallas_call(
        _matmul_res_kernel,
        grid=(n // tm,),
        in_specs=[
            pl.BlockSpec((tm, k), lambda i: (i, 0)),
            pl.BlockSpec((1, k, n_out), lambda i: (layer, 0, 0)),
            pl.BlockSpec((tm, n_out), lambda i: (i, 0)),
        ],
        out_specs=pl.BlockSpec((tm, n_out), lambda i: (i, 0)),
        out_shape=jax.ShapeDtypeStruct((n, n_out), F32),
        compiler_params=_cparams(("parallel",)),
        name="matmul_residual",
    )(a_bf16, w_stack, res)


def _cumsum_rows(x, row):
    n = x.shape[0]
    s = 1
    while s < n:
        x = x + jnp.where(row >= s, pltpu.roll(x, s, axis=0), 0.0)
        s *= 2
    return x


def _hgrn_finish(o, gate_raw, gw):
    o = o * lax.rsqrt(jnp.mean(o * o, axis=-1, keepdims=True) + EPS) * gw
    return o * (gate_raw * _sigmoid(gate_raw))


def _hgrn_exact_head(q, k, v, g, state_t):
    c = HGRN_CHUNK
    qg = (q * jnp.exp2(g)).astype(BF16)
    o_inter = _dot_nt(qg, state_t.astype(BF16))
    nv = c // 8
    g8 = [g[8 * j:8 * (j + 1), :] for j in range(nv)]
    q8 = [q[8 * j:8 * (j + 1), :] for j in range(nv)]
    acc8 = [o_inter[8 * j:8 * (j + 1), :] for j in range(nv)]
    row8 = lax.broadcasted_iota(jnp.int32, (8, HGRN_DK), 0)
    for s in range(c):
        j0, r = divmod(s, 8)
        gs = g8[j0][r:r + 1, :]
        ks = k[s:s + 1, :]
        vs = v[s:s + 1, :]
        for j in range(j0, nv):
            diff = g8[j] - gs
            if j == j0 and r > 0:
                m = row8 >= r
                p = jnp.where(m, q8[j] * jnp.exp2(jnp.where(m, diff, 0.0)) * ks, 0.0)
            else:
                p = q8[j] * jnp.exp2(diff) * ks
            cs = jnp.sum(p, axis=-1, keepdims=True)
            acc8[j] = acc8[j] + cs * vs
    o = jnp.concatenate(acc8, axis=0)
    g_last = g[c - 1:c, :]
    kd = (k * jnp.exp2(g_last - g)).astype(BF16)
    new_state_t = jnp.exp2(g_last) * state_t + _dot_tn(v.astype(BF16), kd)
    return o, new_state_t


def _hgrn_kernel(q_ref, f_ref, i_ref, g_ref, lb_ref, gw_ref, o_ref, state_ref, qs_ref, ks_ref, gs_ref,
                 *, n_chunks):
    @pl.when(pl.program_id(1) == 0)
    def _():
        state_ref[...] = jnp.zeros_like(state_ref)

    c = HGRN_CHUNK
    half = c // 2
    dk = HGRN_DK
    lb = lb_ref[...]
    gw = gw_ref[...]
    row = lax.broadcasted_iota(jnp.int32, (c, D_MODEL), 0)
    t_idx = lax.broadcasted_iota(jnp.int32, (c, 2 * c), 0)
    col = lax.broadcasted_iota(jnp.int32, (c, 2 * c), 1)
    upper = jnp.where(t_idx >= half, t_idx, t_idx + c)
    lower = jnp.where(t_idx >= half, 0, c)
    score_mask = jnp.logical_and(col >= lower, col <= upper)
    pad_rows = jnp.zeros((half, D_MODEL), BF16)

    def body(ci, carry):
        rows = pl.ds(pl.multiple_of(ci * c, c), c)
        qr = q_ref[rows, :].astype(F32)
        fr = f_ref[rows, :]
        q = qr * _sigmoid(qr)
        sig = _sigmoid(fr)
        f = lb + (1.0 - lb) * sig
        k = (1.0 - lb) * (1.0 - sig)
        g = _cumsum_rows(jnp.log2(jnp.maximum(f, F_FLOOR)), row)
        qs_ref[...] = q
        ks_ref[...] = k
        gs_ref[...] = g
        g_mid = g[half - 1:half, :]
        g_last = g[c - 1:c, :]
        span = jnp.maximum(jnp.max(-g_mid), jnp.max(g_mid - g_last))

        @pl.when(span <= HGRN_MAX_FACTORED_SPAN)
        def _():
            q = qs_ref[...]
            k = ks_ref[...]
            g = gs_ref[...]
            qg = (q * jnp.exp2(g)).astype(BF16)
            qb = (q[half:, :] * jnp.exp2(g[half:, :] - g_mid)).astype(BF16)
            kb_f32 = k * jnp.exp2(g_mid - g)
            ka = (kb_f32[:half, :] * jnp.exp2(-g_mid)).astype(BF16)
            kd = (kb_f32 * jnp.exp2(g_last - g_mid)).astype(BF16)
            dl = jnp.exp2(g_last)
            vb = i_ref[rows, :]
            gate_raw = g_ref[rows, :].astype(F32)
            q_mix = jnp.concatenate([qg[:half, :], qb], axis=0)
            k_stack = jnp.concatenate([kb_f32.astype(BF16), ka, pad_rows], axis=0)
            v_stack = jnp.concatenate([vb, vb[:half, :], pad_rows], axis=0)
            heads = [slice(h * dk, (h + 1) * dk) for h in range(HGRN_HEADS)]
            states = [state_ref[h] for h in range(HGRN_HEADS)]
            scores = [_dot_nt(q_mix[:, sl], k_stack[:, sl]) for sl in heads]
            inter = [_dot_nt(qg[:, sl], st.astype(BF16)) for sl, st in zip(heads, states)]
            upd = [_dot_tn(vb[:, sl], kd[:, sl]) for sl in heads]
            scores = [jnp.where(score_mask, s, 0.0).astype(BF16) for s in scores]
            outs = [oi + jnp.dot(s, v_stack[:, sl], preferred_element_type=F32)
                    for oi, s, sl in zip(inter, scores, heads)]
            for h, sl in enumerate(heads):
                state_ref[h] = dl[:, sl] * states[h] + upd[h]
                o_ref[rows, sl] = _hgrn_finish(outs[h], gate_raw[:, sl], gw).astype(o_ref.dtype)

        @pl.when(jnp.logical_not(span <= HGRN_MAX_FACTORED_SPAN))
        def _():
            def head_body(h, hc):
                cols = pl.ds(pl.multiple_of(h * dk, dk), dk)
                o, new_state = _hgrn_exact_head(qs_ref[:, cols], ks_ref[:, cols], i_ref[rows, cols].astype(F32),
                                                gs_ref[:, cols], state_ref[h])
                state_ref[h] = new_state
                o_ref[rows, cols] = _hgrn_finish(o, g_ref[rows, cols].astype(F32), gw).astype(o_ref.dtype)
                return hc

            lax.fori_loop(0, HGRN_HEADS, head_body, 0)

        return carry

    lax.fori_loop(0, n_chunks, body, 0)


def _hgrn_in_proj_kernel(x_ref, nw_ref, w_ref, qig_ref, f_ref, h_ref):
    j = pl.program_id(1)

    @pl.when(j == 0)
    def _():
        h_ref[...] = _rmsnorm_rows(x_ref[...], nw_ref[...]).astype(BF16)

    acc = jnp.dot(h_ref[...], w_ref[0].astype(BF16), preferred_element_type=F32)

    @pl.when(j == 1)
    def _():
        f_ref[...] = acc

    @pl.when(j != 1)
    def _():
        qig_ref[...] = acc.astype(qig_ref.dtype)


def hgrn_in_proj(x, norm_w, w_stack, layer, tm=1024):
    n, d = x.shape
    assert w_stack.shape[2] == 4 * d
    return pl.pallas_call(
        _hgrn_in_proj_kernel,
        grid=(n // tm, 4),
        in_specs=[
            pl.BlockSpec((tm, d), lambda i, j: (i, 0)),
            pl.BlockSpec((1, d), lambda i, j: (0, 0)),
            pl.BlockSpec((1, d, d), lambda i, j: (layer, 0, j)),
        ],
        out_specs=[pl.BlockSpec((tm, d), lambda i, j: (i, jnp.maximum(j - 1, 0))),
                   pl.BlockSpec((tm, d), lambda i, j: (i, 0))],
        out_shape=[jax.ShapeDtypeStruct((n, 3 * d), BF16), jax.ShapeDtypeStruct((n, d), F32)],
        scratch_shapes=[pltpu.VMEM((tm, d), BF16)],
        compiler_params=_cparams(("parallel", "arbitrary")),
        name="hgrn_in_proj",
    )(x, norm_w.reshape(1, d), w_stack)


def hgrn_recurrence(qig, f_raw, lb, gnorm_w, bsz, seq, block_len=256):
    n = bsz * seq
    nl = seq // block_len
    d = D_MODEL

    def col_spec(part):
        return pl.BlockSpec((block_len, d), lambda b, l: (b * nl + l, part))

    return pl.pallas_call(
        functools.partial(_hgrn_kernel, n_chunks=block_len // HGRN_CHUNK),
        grid=(bsz, nl),
        in_specs=[col_spec(0), col_spec(0), col_spec(1), col_spec(2),
                  pl.BlockSpec((1, d), lambda b, l: (0, 0)),
                  pl.BlockSpec((1, HGRN_DK), lambda b, l: (0, 0))],
        out_specs=pl.BlockSpec((block_len, d), lambda b, l: (b * nl + l, 0)),
        out_shape=jax.ShapeDtypeStruct((n, d), BF16),
        scratch_shapes=[pltpu.VMEM((HGRN_HEADS, HGRN_DK, HGRN_DK), F32),
                        pltpu.VMEM((HGRN_CHUNK, d), F32),
                        pltpu.VMEM((HGRN_CHUNK, d), F32),
                        pltpu.VMEM((HGRN_CHUNK, d), F32)],
        compiler_params=_cparams(("parallel", "arbitrary")),
        name="hgrn_recurrence",
    )(qig, f_raw, qig, qig, lb.reshape(1, d), gnorm_w.reshape(1, HGRN_DK))


def _sb_kernel(q_ref, k_ref, v_ref, o_ref, acc_ref, rem_ref, *, tq, n_pairs):
    i = pl.program_id(2)
    pw = 2 * SB_HEAD_DIM
    n_heads = 2 * n_pairs
    first_head = lax.broadcasted_iota(jnp.int32, (tq, pw), 1) < SB_HEAD_DIM
    q_pairs = []
    for p in range(n_pairs):
        q = q_ref[:, p * pw:(p + 1) * pw]
        zero = jnp.zeros_like(q)
        q_pairs.append(jnp.concatenate([jnp.where(first_head, q, zero), jnp.where(first_head, zero, q)], axis=0))
    diag_mask = (lax.broadcasted_iota(jnp.int32, (n_heads * tq, tq), 1)
                 < lax.broadcasted_iota(jnp.int32, (n_heads * tq, tq), 0) % tq)
    later = (lax.broadcasted_iota(jnp.int32, (tq, tq), 0)
             > lax.broadcasted_iota(jnp.int32, (tq, tq), 1)).astype(BF16)

    def sweep_block(kb, masked):
        k0 = pl.multiple_of(kb * tq, tq)
        z = jnp.concatenate(
            [_dot_nt(q_pairs[p], k_ref[pl.ds(k0, tq), p * pw:(p + 1) * pw]) for p in range(n_pairs)],
            axis=0)
        soft = jnp.log2(1.0 + jnp.exp2(jnp.minimum(z, -z)))
        log_beta = jnp.minimum(z, 0.0) - soft
        log_rem = log_beta - z
        if masked:
            log_rem = jnp.where(diag_mask, log_rem, 0.0)
        between = jnp.dot(log_rem.astype(BF16), later, preferred_element_type=F32)
        if masked:
            w = jnp.exp2(jnp.where(diag_mask, log_beta + between, -jnp.inf)).astype(BF16)
            rem = jnp.broadcast_to(jnp.sum(log_rem, axis=-1, keepdims=True), rem_ref.shape)
        else:
            rem = rem_ref[...]
            w = jnp.exp2(log_beta + between + jnp.concatenate([rem] * (tq // rem.shape[1]), axis=1)).astype(BF16)
            rem = rem + jnp.sum(log_rem, axis=-1, keepdims=True)
        rem_ref[...] = rem
        for p in range(n_pairs):
            rows = slice(2 * p * tq, 2 * (p + 1) * tq)
            pv = jnp.dot(w[rows, :], v_ref[pl.ds(k0, tq), p * pw:(p + 1) * pw], preferred_element_type=F32)
            if masked:
                acc_ref[rows, :] = pv
            else:
                acc_ref[rows, :] += pv
        return jnp.max(rem)

    def cond(carry):
        kb, rem_max = carry
        return jnp.logical_and(kb >= 0, rem_max > EXP2_ZERO_BELOW)

    def body(carry):
        kb, _ = carry
        return kb - 1, sweep_block(kb, masked=False)

    lax.while_loop(cond, body, (i - 1, sweep_block(i, masked=True)))
    for p in range(n_pairs):
        first = acc_ref[2 * p * tq:(2 * p + 1) * tq, :]
        second = acc_ref[(2 * p + 1) * tq:(2 * p + 2) * tq, :]
        o_ref[:, p * pw:(p + 1) * pw] = jnp.where(first_head, first, second).astype(o_ref.dtype)


def stick_breaking(q, kv, bsz, seq, tq=256, n_pairs=2):
    n = bsz * seq
    nq = seq // tq
    groups = SB_HEADS // (2 * n_pairs)
    w = 2 * SB_HEAD_DIM * n_pairs
    return pl.pallas_call(
        functools.partial(_sb_kernel, tq=tq, n_pairs=n_pairs),
        grid=(bsz, groups, nq),
        in_specs=[
            pl.BlockSpec((tq, w), lambda b, p, i: (b * nq + i, p)),
            pl.BlockSpec((seq, w), lambda b, p, i: (b, p)),
            pl.BlockSpec((seq, w), lambda b, p, i: (b, groups + p)),
        ],
        out_specs=pl.BlockSpec((tq, w), lambda b, p, i: (b * nq + i, p)),
        out_shape=jax.ShapeDtypeStruct((n, D_MODEL), BF16),
        scratch_shapes=[pltpu.VMEM((2 * n_pairs * tq, 2 * SB_HEAD_DIM), F32),
                        pltpu.VMEM((2 * n_pairs * tq, 2 * SB_HEAD_DIM), F32)],
        compiler_params=_cparams(("parallel", "parallel", "arbitrary")),
        name="stick_breaking",
    )(q, kv, kv)


def _swiglu_kernel(x_ref, nw_ref, wg_ref, wu_ref, wd_ref, o_ref, h_ref, acc_ref):
    j = pl.program_id(1)

    @pl.when(j == 0)
    def _():
        x = x_ref[...]
        h_ref[...] = _rmsnorm_rows(x, nw_ref[...]).astype(BF16)
        acc_ref[...] = x

    h = h_ref[...]
    gate = jnp.dot(h, wg_ref[...], preferred_element_type=F32)
    up = jnp.dot(h, wu_ref[...], preferred_element_type=F32)
    act = (gate * _sigmoid(gate) * up).astype(BF16)
    acc_ref[...] += jnp.dot(act, wd_ref[...], preferred_element_type=F32)

    @pl.when(j == pl.num_programs(1) - 1)
    def _():
        o_ref[...] = acc_ref[...]


def swiglu_residual(x, norm_w, w_gu_bf16, w_down_bf16, tm=512, tf=1408):
    n, d = x.shape
    f = w_down_bf16.shape[0]
    nf = f // tf
    return pl.pallas_call(
        _swiglu_kernel,
        grid=(n // tm, nf),
        in_specs=[
            pl.BlockSpec((tm, d), lambda i, j: (i, 0)),
            pl.BlockSpec((1, d), lambda i, j: (0, 0)),
            pl.BlockSpec((d, tf), lambda i, j: (0, j)),
            pl.BlockSpec((d, tf), lambda i, j: (0, nf + j)),
            pl.BlockSpec((tf, d), lambda i, j: (j, 0)),
        ],
        out_specs=pl.BlockSpec((tm, d), lambda i, j: (i, 0)),
        out_shape=jax.ShapeDtypeStruct((n, d), F32),
        scratch_shapes=[pltpu.VMEM((tm, d), BF16), pltpu.VMEM((tm, d), F32)],
        compiler_params=_cparams(("parallel", "arbitrary")),
        name="swiglu_residual",
    )(x, norm_w.reshape(1, d), w_gu_bf16, w_gu_bf16, w_down_bf16)


ROUTER_LANES = 128


def _router_kernel(x_ref, nw_ref, r_ref, l_ref):
    h = _rmsnorm_rows(x_ref[...], nw_ref[...])
    l_ref[...] = jnp.dot(h, r_ref[...], preferred_element_type=F32, precision=lax.Precision.HIGHEST)


def router_logits(x, norm_w, router, tm=1024):
    n, d = x.shape
    r_pad = jnp.zeros((d, ROUTER_LANES), F32).at[:, :N_EXPERTS].set(router)
    return pl.pallas_call(
        _router_kernel,
        grid=(n // tm,),
        in_specs=[
            pl.BlockSpec((tm, d), lambda i: (i, 0)),
            pl.BlockSpec((1, d), lambda i: (0, 0)),
            pl.BlockSpec((d, ROUTER_LANES), lambda i: (0, 0)),
        ],
        out_specs=pl.BlockSpec((tm, ROUTER_LANES), lambda i: (i, 0)),
        out_shape=jax.ShapeDtypeStruct((n, ROUTER_LANES), F32),
        compiler_params=_cparams(("parallel",)),
        name="moe_router",
    )(x, norm_w.reshape(1, d), r_pad)


def _moe_kernel(be_ref, nv_ref, nu_ref, tok_ref, tok_next_ref, dst_ref, x_hbm, nw_ref, wg_ref, wu_ref, wd_ref,
                out_hbm, xbuf, hbuf, acc_ref, ybuf, gsem, ssem, *, tm):
    i = pl.program_id(0)
    j = pl.program_id(1)
    last_j = pl.num_programs(1) - 1
    n_used = nu_ref[0]
    used = i < n_used

    def gather_copy(tok, slot, g, u):
        return pltpu.make_async_copy(x_hbm.at[pl.ds(tok, 1), :], xbuf.at[slot, g, pl.ds(u, 1), :], gsem.at[slot])

    def scatter_copy(dst, g, u):
        return pltpu.make_async_copy(ybuf.at[g, pl.ds(u, 1), :], out_hbm.at[pl.ds(dst, 1), :], ssem)

    def for_rows(count, issue):
        n_groups = lax.shift_right_logical(count, 3)

        def group(g, c):
            for u in range(8):
                issue(g * 8 + u, g, u)
            return c

        def single(r, c):
            issue(r, n_groups, r - n_groups * 8)
            return c

        lax.fori_loop(0, n_groups, group, 0)
        lax.fori_loop(n_groups * 8, count, single, 0)

    def start_gather(idx_ref, slot, count):
        for_rows(count, lambda r, g, u: gather_copy(idx_ref[0, 0, r], slot, g, u).start())

    def wait_rows(buf, sem, count):
        p = tm
        while p >= 1:
            part = buf.at[pl.ds(0, p // 8)] if p >= 8 else buf.at[0, pl.ds(0, p), :]

            @pl.when((count & p) != 0)
            def _(part=part):
                pltpu.make_async_copy(part, part, sem).wait()

            p //= 2

    @pl.when(jnp.logical_and(used, j == 0))
    def _():
        slot = lax.rem(i, 2)

        @pl.when(i == 0)
        def _():
            xbuf[...] = jnp.zeros_like(xbuf)
            start_gather(tok_ref, 0, nv_ref[0])

        wait_rows(xbuf.at[slot], gsem.at[slot], nv_ref[i])

        @pl.when(i + 1 < n_used)
        def _():
            start_gather(tok_next_ref, 1 - slot, nv_ref[i + 1])

        x = xbuf[slot].reshape(tm, xbuf.shape[-1])
        hbuf[...] = _rmsnorm_rows(x, nw_ref[...]).astype(BF16)

    @pl.when(used)
    def _():
        h = hbuf[...]
        gate = jnp.dot(h, wg_ref[0, 0].astype(BF16), preferred_element_type=F32)
        up = jnp.dot(h, wu_ref[0, 0].astype(BF16), preferred_element_type=F32)
        act = (gate * _sigmoid(gate) * up).astype(BF16)
        part = jnp.dot(act, wd_ref[0, 0].astype(BF16), preferred_element_type=F32)

        @pl.when(j == 0)
        def _():
            acc_ref[...] = part

        @pl.when(jnp.logical_and(j > 0, j < last_j))
        def _():
            acc_ref[...] += part

        @pl.when(j == last_j)
        def _():
            @pl.when(i > 0)
            def _():
                wait_rows(ybuf, ssem, nv_ref[i - 1])

            ybuf[...] = (acc_ref[...] + part).reshape(ybuf.shape)

            for_rows(nv_ref[i], lambda r, g, u: scatter_copy(dst_ref[0, 0, r], g, u).start())

            @pl.when(i == n_used - 1)
            def _():
                wait_rows(ybuf, ssem, nv_ref[i])


def moe_experts(x, norm_w, tok, dst, block_expert, n_valid, n_used, wg, wu, wd, moe_idx, tm, tf=512):
    n, d = x.shape
    f = wg.shape[3]
    n_blocks = tok.shape[0]
    nj = f // tf

    def j_eff(i, j, nu):
        return jnp.where(i < nu[0], j, nj - 1)

    grid_spec = pltpu.PrefetchScalarGridSpec(
        num_scalar_prefetch=3,
        grid=(n_blocks, nj),
        in_specs=[
            pl.BlockSpec((1, 1, tm), lambda i, j, be, nv, nu: (i, 0, 0), memory_space=pltpu.SMEM),
            pl.BlockSpec((1, 1, tm), lambda i, j, be, nv, nu: (jnp.minimum(i + 1, n_blocks - 1), 0, 0),
                         memory_space=pltpu.SMEM),
            pl.BlockSpec((1, 1, tm), lambda i, j, be, nv, nu: (i, 0, 0), memory_space=pltpu.SMEM),
            pl.BlockSpec(memory_space=pl.ANY),
            pl.BlockSpec((1, d), lambda i, j, be, nv, nu: (0, 0)),
            pl.BlockSpec((1, 1, d, tf), lambda i, j, be, nv, nu: (moe_idx, be[i], 0, j_eff(i, j, nu))),
            pl.BlockSpec((1, 1, d, tf), lambda i, j, be, nv, nu: (moe_idx, be[i], 0, j_eff(i, j, nu))),
            pl.BlockSpec((1, 1, tf, d), lambda i, j, be, nv, nu: (moe_idx, be[i], j_eff(i, j, nu), 0)),
        ],
        out_specs=pl.BlockSpec(memory_space=pl.ANY),
        scratch_shapes=[
            pltpu.VMEM((2, tm // 8, 8, d), F32),
            pltpu.VMEM((tm, d), BF16),
            pltpu.VMEM((tm, d), F32),
            pltpu.VMEM((tm // 8, 8, d), F32),
            pltpu.SemaphoreType.DMA((2,)),
            pltpu.SemaphoreType.DMA(()),
        ],
    )
    return pl.pallas_call(
        functools.partial(_moe_kernel, tm=tm),
        grid_spec=grid_spec,
        out_shape=jax.ShapeDtypeStruct((TOP_K * n, d), F32),
        compiler_params=_cparams(("arbitrary", "arbitrary")),
        name="moe_experts",
    )(block_expert, n_valid, n_used, tok, tok, dst, x, norm_w.reshape(1, d), wg, wu, wd)


def _combine_kernel(x_ref, y0_ref, y1_ref, g_ref, o_ref):
    g = g_ref[...]
    o_ref[...] = x_ref[...] + g[:, 0:1] * y0_ref[0] + g[:, 1:2] * y1_ref[0]


def _combine_norm_kernel(x_ref, y0_ref, y1_ref, g_ref, w_ref, o_ref):
    g = g_ref[...]
    o_ref[...] = _rmsnorm_rows(x_ref[...] + g[:, 0:1] * y0_ref[0] + g[:, 1:2] * y1_ref[0], w_ref[...])


def moe_combine(x, y, gates, out_norm_w=None, tm=512):
    n, d = x.shape
    in_specs = [pl.BlockSpec((tm, d), lambda i: (i, 0)),
                pl.BlockSpec((1, tm, d), lambda i: (0, i, 0)),
                pl.BlockSpec((1, tm, d), lambda i: (1, i, 0)),
                pl.BlockSpec((tm, TOP_K), lambda i: (i, 0))]
    args = [x, y, y, gates]
    body = _combine_kernel
    if out_norm_w is not None:
        in_specs.append(pl.BlockSpec((1, d), lambda i: (0, 0)))
        args.append(out_norm_w.reshape(1, d))
        body = _combine_norm_kernel
    return pl.pallas_call(
        body,
        grid=(n // tm,),
        in_specs=in_specs,
        out_specs=pl.BlockSpec((tm, d), lambda i: (i, 0)),
        out_shape=jax.ShapeDtypeStruct((n, d), F32),
        compiler_params=_cparams(("parallel",)),
        name="moe_combine",
    )(*args)


def moe_layer(x, norm_w, router, wg, wu, wd, moe_idx, out_norm_w=None, tm=1024):
    n, d = x.shape
    n_assign = n * TOP_K
    logits = router_logits(x, norm_w, router)
    top_logits, top_idx = lax.top_k(logits[:, :N_EXPERTS], TOP_K)
    gates = jax.nn.softmax(top_logits, axis=-1)

    flat_e = top_idx.reshape(-1).astype(jnp.int32)
    order = jnp.argsort(flat_e).astype(jnp.int32)
    counts = jnp.sum((flat_e[:, None] == jnp.arange(N_EXPERTS, dtype=jnp.int32)[None, :]).astype(jnp.int32), axis=0)
    starts = jnp.cumsum(counts) - counts
    n_blk_e = (counts + tm - 1) // tm
    blk_ends = jnp.cumsum(n_blk_e)
    blk_starts = blk_ends - n_blk_e
    n_blocks = n_assign // tm + N_EXPERTS
    blk = jnp.arange(n_blocks, dtype=jnp.int32)
    block_expert = jnp.minimum(jnp.sum((blk[:, None] >= blk_ends[None, :]).astype(jnp.int32), axis=1),
                               N_EXPERTS - 1)
    n_used = blk_ends[-1].astype(jnp.int32).reshape(1)
    row0 = (blk - blk_starts[block_expert]) * tm
    n_valid = jnp.where(blk < n_used[0], jnp.clip(counts[block_expert] - row0, 0, tm), 0).astype(jnp.int32)
    pos = starts[block_expert][:, None] + row0[:, None] + jnp.arange(tm, dtype=jnp.int32)[None, :]
    asg = jnp.take(order, jnp.clip(pos, 0, n_assign - 1), axis=0).reshape(n_blocks, 1, tm)
    tok = lax.shift_right_logical(asg, 1)
    dst = (asg & 1) * n + tok

    y = moe_experts(x, norm_w, tok, dst, block_expert, n_valid, n_used, wg, wu, wd, moe_idx, tm)
    return moe_combine(x, y.reshape(TOP_K, n, d), gates, out_norm_w)


def kernel(x, mix_norm, ffn_norm, hgrn_w_in, hgrn_lb_raw, hgrn_gnorm, hgrn_w_out, kv_norm, w_kv, sb_w_q, sb_w_o, ffn_w_gu, ffn_w_down, moe_router, moe_w_gate, moe_w_up, moe_w_down, final_norm):
    bsz, seq, d = x.shape
    n = bsz * seq
    depth = mix_norm.shape[0]
    n_a = hgrn_w_in.shape[0]
    assert depth % 2 == 0, "the final rmsnorm is fused into the last (expert) layer's combine"
    xs = x.reshape(n, d)

    p = jax.nn.softmax(hgrn_lb_raw.astype(F32), axis=0)
    lower_bounds = jnp.cumsum(p, axis=0) - p[0:1]

    kv = None
    for layer in range(depth):
        if layer < n_a:
            qig, f_raw = hgrn_in_proj(xs, mix_norm[layer], hgrn_w_in, layer)
            og = hgrn_recurrence(qig, f_raw, lower_bounds[layer], hgrn_gnorm[layer], bsz, seq)
            xs = matmul_residual(og, hgrn_w_out, layer, xs)
        else:
            j = layer - n_a
            q = norm_matmul(xs, mix_norm[layer], sb_w_q, j, BF16, out_scale=SB_LOGIT_SCALE_LOG2)
            att = stick_breaking(q, kv, bsz, seq)
            xs = matmul_residual(att, sb_w_o, j, xs)
        if layer % 2 == 0:
            xs = swiglu_residual(xs, ffn_norm[layer], ffn_w_gu[layer // 2].astype(BF16),
                                 ffn_w_down[layer // 2].astype(BF16))
        else:
            e = layer // 2
            xs = moe_layer(xs, ffn_norm[layer], moe_router[e], moe_w_gate, moe_w_up, moe_w_down, e,
                           out_norm_w=final_norm if layer == depth - 1 else None)
        if layer == n_a - 1:
            kv = norm_matmul(xs, kv_norm, w_kv.reshape(1, d, 2 * d), 0, BF16)
    return xs.reshape(bsz, seq, d)
```

```python
import functools

import jax
import jax.numpy as jnp
from jax import lax
from jax.experimental import pallas as pl
from jax.experimental.pallas import tpu as pltpu

F32 = jnp.float32
BF16 = jnp.bfloat16

D_MODEL = 1024
HGRN_HEADS = 8
HGRN_DK = 128
HGRN_CHUNK = 64
SB_HEADS = 16
SB_HEAD_DIM = 64
N_EXPERTS = 8
TOP_K = 2
EPS = 1e-6
F_FLOOR = 1e-30
LOG2E = 1.4426950408889634

VMEM_LIMIT_BYTES = 52 * 1024 * 1024
EXP2_ZERO_BELOW = -150.0
HGRN_MAX_FACTORED_SPAN = 100.0
SB_LOGIT_SCALE_LOG2 = (SB_HEAD_DIM ** -0.5) * LOG2E


def _cparams(sem):
    return pltpu.CompilerParams(dimension_semantics=sem, vmem_limit_bytes=VMEM_LIMIT_BYTES)


def _sigmoid(x):
    return 0.5 * jnp.tanh(0.5 * x) + 0.5


def _rmsnorm_rows(x, w):
    ms = jnp.mean(x * x, axis=-1, keepdims=True)
    return x * lax.rsqrt(ms + EPS) * w


def _dot_nt(a, b):
    return lax.dot_general(a, b, (((1,), (1,)), ((), ())), preferred_element_type=F32)


def _dot_tn(a, b):
    return lax.dot_general(a, b, (((0,), (0,)), ((), ())), preferred_element_type=F32)


def _norm_matmul_kernel(x_ref, nw_ref, w_ref, o_ref, h_ref, *, out_scale):
    @pl.when(pl.program_id(1) == 0)
    def _():
        h_ref[...] = _rmsnorm_rows(x_ref[...], nw_ref[...]).astype(BF16)

    acc = jnp.dot(h_ref[...], w_ref[0].astype(BF16), preferred_element_type=F32)
    if out_scale is not None:
        acc = acc * out_scale
    o_ref[...] = acc.astype(o_ref.dtype)


def norm_matmul(x, norm_w, w_stack, layer, out_dtype, out_scale=None, tm=1024, tn=1024):
    n, d = x.shape
    n_out = w_stack.shape[2]
    return pl.pallas_call(
        functools.partial(_norm_matmul_kernel, out_scale=out_scale),
        grid=(n // tm, n_out // tn),
        in_specs=[
            pl.BlockSpec((tm, d), lambda i, j: (i, 0)),
            pl.BlockSpec((1, d), lambda i, j: (0, 0)),
            pl.BlockSpec((1, d, tn), lambda i, j: (layer, 0, j)),
        ],
        out_specs=pl.BlockSpec((tm, tn), lambda i, j: (i, j)),
        out_shape=jax.ShapeDtypeStruct((n, n_out), out_dtype),
        scratch_shapes=[pltpu.VMEM((tm, d), BF16)],
        compiler_params=_cparams(("parallel", "arbitrary")),
        name="norm_matmul",
    )(x, norm_w.reshape(1, d), w_stack)


def _matmul_res_kernel(a_ref, w_ref, r_ref, o_ref):
    o_ref[...] = r_ref[...] + jnp.dot(a_ref[...], w_ref[0].astype(BF16), preferred_element_type=F32)


def matmul_residual(a_bf16, w_stack, layer, res, tm=1024):
    n, k = a_bf16.shape
    n_out = w_stack.shape[2]
    return pl.pallas_call(
        _matmul_res_kernel,
        grid=(n // tm,),
        in_specs=[
            pl.BlockSpec((tm, k), lambda i: (i, 0)),
            pl.BlockSpec((1, k, n_out), lambda i: (layer, 0, 0)),
            pl.BlockSpec((tm, n_out), lambda i: (i, 0)),
        ],
        out_specs=pl.BlockSpec((tm, n_out), lambda i: (i, 0)),
        out_shape=jax.ShapeDtypeStruct((n, n_out), F32),
        compiler_params=_cparams(("parallel",)),
        name="matmul_residual",
    )(a_bf16, w_stack, res)


def _cumsum_rows(x, row):
    n = x.shape[0]
    s = 1
    while s < n:
        x = x + jnp.where(row >= s, pltpu.roll(x, s, axis=0), 0.0)
        s *= 2
    return x


def _hgrn_finish(o, gate_raw, gw):
    o = o * lax.rsqrt(jnp.mean(o * o, axis=-1, keepdims=True) + EPS) * gw
    return o * (gate_raw * _sigmoid(gate_raw))


def _hgrn_exact_head(q, k, v, g, state_t):
    c = HGRN_CHUNK
    qg = (q * jnp.exp2(g)).astype(BF16)
    o_inter = _dot_nt(qg, state_t.astype(BF16))
    nv = c // 8
    g8 = [g[8 * j:8 * (j + 1), :] for j in range(nv)]
    q8 = [q[8 * j:8 * (j + 1), :] for j in range(nv)]
    acc8 = [o_inter[8 * j:8 * (j + 1), :] for j in range(nv)]
    row8 = lax.broadcasted_iota(jnp.int32, (8, HGRN_DK), 0)
    for s in range(c):
        j0, r = divmod(s, 8)
        gs = g8[j0][r:r + 1, :]
        ks = k[s:s + 1, :]
        vs = v[s:s + 1, :]
        for j in range(j0, nv):
            diff = g8[j] - gs
            if j == j0 and r > 0:
                m = row8 >= r
                p = jnp.where(m, q8[j] * jnp.exp2(jnp.where(m, diff, 0.0)) * ks, 0.0)
            else:
                p = q8[j] * jnp.exp2(diff) * ks
            cs = jnp.sum(p, axis=-1, keepdims=True)
            acc8[j] = acc8[j] + cs * vs
    o = jnp.concatenate(acc8, axis=0)
    g_last = g[c - 1:c, :]
    kd = (k * jnp.exp2(g_last - g)).astype(BF16)
    new_state_t = jnp.exp2(g_last) * state_t + _dot_tn(v.astype(BF16), kd)
    return o, new_state_t


def _hgrn_kernel(q_ref, f_ref, i_ref, g_ref, lb_ref, gw_ref, o_ref, state_ref, qs_ref, ks_ref, gs_ref,
                 *, n_chunks):
    @pl.when(pl.program_id(1) == 0)
    def _():
        state_ref[...] = jnp.zeros_like(state_ref)

    c = HGRN_CHUNK
    half = c // 2
    dk = HGRN_DK
    lb = lb_ref[...]
    gw = gw_ref[...]
    row = lax.broadcasted_iota(jnp.int32, (c, D_MODEL), 0)
    t_idx = lax.broadcasted_iota(jnp.int32, (c, 2 * c), 0)
    col = lax.broadcasted_iota(jnp.int32, (c, 2 * c), 1)
    upper = jnp.where(t_idx >= half, t_idx, t_idx + c)
    lower = jnp.where(t_idx >= half, 0, c)
    score_mask = jnp.logical_and(col >= lower, col <= upper)
    pad_rows = jnp.zeros((half, D_MODEL), BF16)

    def body(ci, carry):
        rows = pl.ds(pl.multiple_of(ci * c, c), c)
        qr = q_ref[rows, :].astype(F32)
        fr = f_ref[rows, :]
        q = qr * _sigmoid(qr)
        sig = _sigmoid(fr)
        f = lb + (1.0 - lb) * sig
        k = (1.0 - lb) * (1.0 - sig)
        g = _cumsum_rows(jnp.log2(jnp.maximum(f, F_FLOOR)), row)
        qs_ref[...] = q
        ks_ref[...] = k
        gs_ref[...] = g
        g_mid = g[half - 1:half, :]
        g_last = g[c - 1:c, :]
        span = jnp.maximum(jnp.max(-g_mid), jnp.max(g_mid - g_last))

        @pl.when(span <= HGRN_MAX_FACTORED_SPAN)
        def _():
            q = qs_ref[...]
            k = ks_ref[...]
            g = gs_ref[...]
            qg = (q * jnp.exp2(g)).astype(BF16)
            qb = (q[half:, :] * jnp.exp2(g[half:, :] - g_mid)).astype(BF16)
            kb_f32 = k * jnp.exp2(g_mid - g)
            ka = (kb_f32[:half, :] * jnp.exp2(-g_mid)).astype(BF16)
            kd = (kb_f32 * jnp.exp2(g_last - g_mid)).astype(BF16)
            dl = jnp.exp2(g_last)
            vb = i_ref[rows, :]
            gate_raw = g_ref[rows, :].astype(F32)
            q_mix = jnp.concatenate([qg[:half, :], qb], axis=0)
            k_stack = jnp.concatenate([kb_f32.astype(BF16), ka, pad_rows], axis=0)
            v_stack = jnp.concatenate([vb, vb[:half, :], pad_rows], axis=0)
            heads = [slice(h * dk, (h + 1) * dk) for h in range(HGRN_HEADS)]
            states = [state_ref[h] for h in range(HGRN_HEADS)]
            scores = [_dot_nt(q_mix[:, sl], k_stack[:, sl]) for sl in heads]
            inter = [_dot_nt(qg[:, sl], st.astype(BF16)) for sl, st in zip(heads, states)]
            upd = [_dot_tn(vb[:, sl], kd[:, sl]) for sl in heads]
            scores = [jnp.where(score_mask, s, 0.0).astype(BF16) for s in scores]
            outs = [oi + jnp.dot(s, v_stack[:, sl], preferred_element_type=F32)
                    for oi, s, sl in zip(inter, scores, heads)]
            for h, sl in enumerate(heads):
                state_ref[h] = dl[:, sl] * states[h] + upd[h]
                o_ref[rows, sl] = _hgrn_finish(outs[h], gate_raw[:, sl], gw).astype(o_ref.dtype)

        @pl.when(jnp.logical_not(span <= HGRN_MAX_FACTORED_SPAN))
        def _():
            def head_body(h, hc):
                cols = pl.ds(pl.multiple_of(h * dk, dk), dk)
                o, new_state = _hgrn_exact_head(qs_ref[:, cols], ks_ref[:, cols], i_ref[rows, cols].astype(F32),
                                                gs_ref[:, cols], state_ref[h])
                state_ref[h] = new_state
                o_ref[rows, cols] = _hgrn_finish(o, g_ref[rows, cols].astype(F32), gw).astype(o_ref.dtype)
                return hc

            lax.fori_loop(0, HGRN_HEADS, head_body, 0)

        return carry

    lax.fori_loop(0, n_chunks, body, 0)


def _hgrn_in_proj_kernel(x_ref, nw_ref, w_ref, qig_ref, f_ref, h_ref):
    j = pl.program_id(1)

    @pl.when(j == 0)
    def _():
        h_ref[...] = _rmsnorm_rows(x_ref[...], nw_ref[...]).astype(BF16)

    acc = jnp.dot(h_ref[...], w_ref[0].astype(BF16), preferred_element_type=F32)

    @pl.when(j == 1)
    def _():
        f_ref[...] = acc

    @pl.when(j != 1)
    def _():
        qig_ref[...] = acc.astype(qig_ref.dtype)


def hgrn_in_proj(x, norm_w, w_stack, layer, tm=1024):
    n, d = x.shape
    assert w_stack.shape[2] == 4 * d
    return pl.pallas_call(
        _hgrn_in_proj_kernel,
        grid=(n // tm, 4),
        in_specs=[
            pl.BlockSpec((tm, d), lambda i, j: (i, 0)),
            pl.BlockSpec((1, d), lambda i, j: (0, 0)),
            pl.BlockSpec((1, d, d), lambda i, j: (layer, 0, j)),
        ],
        out_specs=[pl.BlockSpec((tm, d), lambda i, j: (i, jnp.maximum(j - 1, 0))),
                   pl.BlockSpec((tm, d), lambda i, j: (i, 0))],
        out_shape=[jax.ShapeDtypeStruct((n, 3 * d), BF16), jax.ShapeDtypeStruct((n, d), F32)],
        scratch_shapes=[pltpu.VMEM((tm, d), BF16)],
        compiler_params=_cparams(("parallel", "arbitrary")),
        name="hgrn_in_proj",
    )(x, norm_w.reshape(1, d), w_stack)


def hgrn_recurrence(qig, f_raw, lb, gnorm_w, bsz, seq, block_len=512):
    n = bsz * seq
    nl = seq // block_len
    d = D_MODEL

    def col_spec(part):
        return pl.BlockSpec((block_len, d), lambda b, l: (b * nl + l, part))

    return pl.pallas_call(
        functools.partial(_hgrn_kernel, n_chunks=block_len // HGRN_CHUNK),
        grid=(bsz, nl),
        in_specs=[col_spec(0), col_spec(0), col_spec(1), col_spec(2),
                  pl.BlockSpec((1, d), lambda b, l: (0, 0)),
                  pl.BlockSpec((1, HGRN_DK), lambda b, l: (0, 0))],
        out_specs=pl.BlockSpec((block_len, d), lambda b, l: (b * nl + l, 0)),
        out_shape=jax.ShapeDtypeStruct((n, d), BF16),
        scratch_shapes=[pltpu.VMEM((HGRN_HEADS, HGRN_DK, HGRN_DK), F32),
                        pltpu.VMEM((HGRN_CHUNK, d), F32),
                        pltpu.VMEM((HGRN_CHUNK, d), F32),
                        pltpu.VMEM((HGRN_CHUNK, d), F32)],
        compiler_params=_cparams(("parallel", "arbitrary")),
        name="hgrn_recurrence",
    )(qig, f_raw, qig, qig, lb.reshape(1, d), gnorm_w.reshape(1, HGRN_DK))


def _sb_kernel(q_ref, k_ref, v_ref, o_ref, acc_ref, rem_ref, *, tq, n_pairs):
    i = pl.program_id(2)
    pw = 2 * SB_HEAD_DIM
    n_heads = 2 * n_pairs
    first_head = lax.broadcasted_iota(jnp.int32, (tq, pw), 1) < SB_HEAD_DIM
    q_pairs = []
    for p in range(n_pairs):
        q = q_ref[:, p * pw:(p + 1) * pw]
        zero = jnp.zeros_like(q)
        q_pairs.append(jnp.concatenate([jnp.where(first_head, q, zero), jnp.where(first_head, zero, q)], axis=0))
    diag_mask = (lax.broadcasted_iota(jnp.int32, (n_heads * tq, tq), 1)
                 < lax.broadcasted_iota(jnp.int32, (n_heads * tq, tq), 0) % tq)
    later = (lax.broadcasted_iota(jnp.int32, (tq, tq), 0)
             > lax.broadcasted_iota(jnp.int32, (tq, tq), 1)).astype(BF16)

    def sweep_block(kb, masked):
        k0 = pl.multiple_of(kb * tq, tq)
        z = jnp.concatenate(
            [_dot_nt(q_pairs[p], k_ref[pl.ds(k0, tq), p * pw:(p + 1) * pw]) for p in range(n_pairs)],
            axis=0)
        soft = jnp.log2(1.0 + jnp.exp2(jnp.minimum(z, -z)))
        log_beta = jnp.minimum(z, 0.0) - soft
        log_rem = log_beta - z
        if masked:
            log_rem = jnp.where(diag_mask, log_rem, 0.0)
        between = jnp.dot(log_rem.astype(BF16), later, preferred_element_type=F32)
        if masked:
            w = jnp.exp2(jnp.where(diag_mask, log_beta + between, -jnp.inf)).astype(BF16)
            rem = jnp.broadcast_to(jnp.sum(log_rem, axis=-1, keepdims=True), rem_ref.shape)
        else:
            rem = rem_ref[...]
            w = jnp.exp2(log_beta + between + jnp.concatenate([rem] * (tq // rem.shape[1]), axis=1)).astype(BF16)
            rem = rem + jnp.sum(log_rem, axis=-1, keepdims=True)
        rem_ref[...] = rem
        for p in range(n_pairs):
            rows = slice(2 * p * tq, 2 * (p + 1) * tq)
            pv = jnp.dot(w[rows, :], v_ref[pl.ds(k0, tq), p * pw:(p + 1) * pw], preferred_element_type=F32)
            if masked:
                acc_ref[rows, :] = pv
            else:
                acc_ref[rows, :] += pv
        return jnp.max(rem)

    def cond(carry):
        kb, rem_max = carry
        return jnp.logical_and(kb >= 0, rem_max > EXP2_ZERO_BELOW)

    def body(carry):
        kb, _ = carry
        return kb - 1, sweep_block(kb, masked=False)

    lax.while_loop(cond, body, (i - 1, sweep_block(i, masked=True)))
    for p in range(n_pairs):
        first = acc_ref[2 * p * tq:(2 * p + 1) * tq, :]
        second = acc_ref[(2 * p + 1) * tq:(2 * p + 2) * tq, :]
        o_ref[:, p * pw:(p + 1) * pw] = jnp.where(first_head, first, second).astype(o_ref.dtype)


def stick_breaking(q, kv, bsz, seq, tq=256, n_pairs=2):
    n = bsz * seq
    nq = seq // tq
    groups = SB_HEADS // (2 * n_pairs)
    w = 2 * SB_HEAD_DIM * n_pairs
    return pl.pallas_call(
        functools.partial(_sb_kernel, tq=tq, n_pairs=n_pairs),
        grid=(bsz, groups, nq),
        in_specs=[
            pl.BlockSpec((tq, w), lambda b, p, i: (b * nq + i, p)),
            pl.BlockSpec((seq, w), lambda b, p, i: (b, p)),
            pl.BlockSpec((seq, w), lambda b, p, i: (b, groups + p)),
        ],
        out_specs=pl.BlockSpec((tq, w), lambda b, p, i: (b * nq + i, p)),
        out_shape=jax.ShapeDtypeStruct((n, D_MODEL), BF16),
        scratch_shapes=[pltpu.VMEM((2 * n_pairs * tq, 2 * SB_HEAD_DIM), F32),
                        pltpu.VMEM((2 * n_pairs * tq, 2 * SB_HEAD_DIM), F32)],
        compiler_params=_cparams(("parallel", "parallel", "arbitrary")),
        name="stick_breaking",
    )(q, kv, kv)


def _swiglu_kernel(x_ref, nw_ref, wg_ref, wu_ref, wd_ref, o_ref, h_ref, acc_ref):
    j = pl.program_id(1)

    @pl.when(j == 0)
    def _():
        x = x_ref[...]
        h_ref[...] = _rmsnorm_rows(x, nw_ref[...]).astype(BF16)
        acc_ref[...] = x

    h = h_ref[...]
    gate = jnp.dot(h, wg_ref[...], preferred_element_type=F32)
    up = jnp.dot(h, wu_ref[...], preferred_element_type=F32)
    act = (gate * _sigmoid(gate) * up).astype(BF16)
    acc_ref[...] += jnp.dot(act, wd_ref[...], preferred_element_type=F32)

    @pl.when(j == pl.num_programs(1) - 1)
    def _():
        o_ref[...] = acc_ref[...]


def swiglu_residual(x, norm_w, w_gu_bf16, w_down_bf16, tm=512, tf=1408):
    n, d = x.shape
    f = w_down_bf16.shape[0]
    nf = f // tf
    return pl.pallas_call(
        _swiglu_kernel,
        grid=(n // tm, nf),
        in_specs=[
            pl.BlockSpec((tm, d), lambda i, j: (i, 0)),
            pl.BlockSpec((1, d), lambda i, j: (0, 0)),
            pl.BlockSpec((d, tf), lambda i, j: (0, j)),
            pl.BlockSpec((d, tf), lambda i, j: (0, nf + j)),
            pl.BlockSpec((tf, d), lambda i, j: (j, 0)),
        ],
        out_specs=pl.BlockSpec((tm, d), lambda i, j: (i, 0)),
        out_shape=jax.ShapeDtypeStruct((n, d), F32),
        scratch_shapes=[pltpu.VMEM((tm, d), BF16), pltpu.VMEM((tm, d), F32)],
        compiler_params=_cparams(("parallel", "arbitrary")),
        name="swiglu_residual",
    )(x, norm_w.reshape(1, d), w_gu_bf16, w_gu_bf16, w_down_bf16)


ROUTER_LANES = 128


def _split_bf16(a):
    hi = a.astype(BF16)
    return hi, (a - hi.astype(F32)).astype(BF16)


def _router_kernel(x_ref, nw_ref, r_ref, idx_ref, gate_ref):
    h_hi, h_lo = _split_bf16(_rmsnorm_rows(x_ref[...], nw_ref[...]))
    r_hi, r_lo = _split_bf16(r_ref[...])
    logits = (jnp.dot(h_hi, r_hi, preferred_element_type=F32) + jnp.dot(h_hi, r_lo, preferred_element_type=F32)
              + jnp.dot(h_lo, r_hi, preferred_element_type=F32))
    lane = lax.broadcasted_iota(jnp.int32, logits.shape, 1)
    lane_f = lane.astype(F32)
    logits = jnp.where(lane < N_EXPERTS, logits, -jnp.inf)
    m1 = jnp.max(logits, axis=-1, keepdims=True)
    i1 = jnp.min(jnp.where(logits == m1, lane_f, float(ROUTER_LANES)), axis=-1, keepdims=True)
    rest = jnp.where(lane_f == i1, -jnp.inf, logits)
    m2 = jnp.max(rest, axis=-1, keepdims=True)
    i2 = jnp.min(jnp.where(rest == m2, lane_f, float(ROUTER_LANES)), axis=-1, keepdims=True)
    e = jnp.exp(m2 - m1)
    g1 = 1.0 / (1.0 + e)
    first = lane == 0
    idx_ref[...] = jnp.where(first, i1, i2)[:, :TOP_K].astype(jnp.int32)
    gate_ref[...] = jnp.where(first, g1, e * g1)[:, :TOP_K]


def router_top2(x, norm_w, router, tm=1024):
    n, d = x.shape
    r_pad = jnp.zeros((d, ROUTER_LANES), F32).at[:, :N_EXPERTS].set(router)
    return pl.pallas_call(
        _router_kernel,
        grid=(n // tm,),
        in_specs=[
            pl.BlockSpec((tm, d), lambda i: (i, 0)),
            pl.BlockSpec((1, d), lambda i: (0, 0)),
            pl.BlockSpec((d, ROUTER_LANES), lambda i: (0, 0)),
        ],
        out_specs=[pl.BlockSpec((tm, TOP_K), lambda i: (i, 0)), pl.BlockSpec((tm, TOP_K), lambda i: (i, 0))],
        out_shape=[jax.ShapeDtypeStruct((n, TOP_K), jnp.int32), jax.ShapeDtypeStruct((n, TOP_K), F32)],
        compiler_params=_cparams(("parallel",)),
        name="moe_router",
    )(x, norm_w.reshape(1, d), r_pad)


def _moe_kernel(be_ref, nv_ref, nu_ref, tok_ref, tok_next_ref, dst_ref, x_hbm, nw_ref, wg_ref, wu_ref, wd_ref,
                out_hbm, xbuf, hbuf, acc_ref, ybuf, gsem, ssem, *, tm):
    i = pl.program_id(0)
    j = pl.program_id(1)
    last_j = pl.num_programs(1) - 1
    n_used = nu_ref[0]
    used = i < n_used

    def gather_copy(tok, slot, g, u):
        return pltpu.make_async_copy(x_hbm.at[pl.ds(tok, 1), :], xbuf.at[slot, g, pl.ds(u, 1), :], gsem.at[slot])

    def scatter_copy(dst, g, u):
        return pltpu.make_async_copy(ybuf.at[g, pl.ds(u, 1), :], out_hbm.at[pl.ds(dst, 1), :], ssem)

    def for_rows(count, issue):
        n_groups = lax.shift_right_logical(count, 3)

        def group(g, c):
            for u in range(8):
                issue(g * 8 + u, g, u)
            return c

        def single(r, c):
            issue(r, n_groups, r - n_groups * 8)
            return c

        lax.fori_loop(0, n_groups, group, 0)
        lax.fori_loop(n_groups * 8, count, single, 0)

    def start_gather(idx_ref, slot, count):
        for_rows(count, lambda r, g, u: gather_copy(idx_ref[0, 0, r], slot, g, u).start())

    def wait_rows(buf, sem, count):
        p = tm
        while p >= 1:
            part = buf.at[pl.ds(0, p // 8)] if p >= 8 else buf.at[0, pl.ds(0, p), :]

            @pl.when((count & p) != 0)
            def _(part=part):
                pltpu.make_async_copy(part, part, sem).wait()

            p //= 2

    @pl.when(jnp.logical_and(used, j == 0))
    def _():
        slot = lax.rem(i, 2)

        @pl.when(i == 0)
        def _():
            xbuf[...] = jnp.zeros_like(xbuf)
            start_gather(tok_ref, 0, nv_ref[0])

        wait_rows(xbuf.at[slot], gsem.at[slot], nv_ref[i])

        @pl.when(i + 1 < n_used)
        def _():
            start_gather(tok_next_ref, 1 - slot, nv_ref[i + 1])

        x = xbuf[slot].reshape(tm, xbuf.shape[-1])
        hbuf[...] = _rmsnorm_rows(x, nw_ref[...]).astype(BF16)

    @pl.when(used)
    def _():
        h = hbuf[...]
        gate = jnp.dot(h, wg_ref[0, 0].astype(BF16), preferred_element_type=F32)
        up = jnp.dot(h, wu_ref[0, 0].astype(BF16), preferred_element_type=F32)
        act = (gate * _sigmoid(gate) * up).astype(BF16)
        part = jnp.dot(act, wd_ref[0, 0].astype(BF16), preferred_element_type=F32)

        @pl.when(j == 0)
        def _():
            acc_ref[...] = part

        @pl.when(jnp.logical_and(j > 0, j < last_j))
        def _():
            acc_ref[...] += part

        @pl.when(j == last_j)
        def _():
            @pl.when(i > 0)
            def _():
                wait_rows(ybuf, ssem, nv_ref[i - 1])

            ybuf[...] = (acc_ref[...] + part).reshape(ybuf.shape)

            for_rows(nv_ref[i], lambda r, g, u: scatter_copy(dst_ref[0, 0, r], g, u).start())

            @pl.when(i == n_used - 1)
            def _():
                wait_rows(ybuf, ssem, nv_ref[i])


def moe_experts(x, norm_w, tok, dst, block_expert, n_valid, n_used, wg, wu, wd, moe_idx, tm, tf=512):
    n, d = x.shape
    f = wg.shape[3]
    n_blocks = tok.shape[0]
    nj = f // tf

    def j_eff(i, j, nu):
        return jnp.where(i < nu[0], j, nj - 1)

    grid_spec = pltpu.PrefetchScalarGridSpec(
        num_scalar_prefetch=3,
        grid=(n_blocks, nj),
        in_specs=[
            pl.BlockSpec((1, 1, tm), lambda i, j, be, nv, nu: (i, 0, 0), memory_space=pltpu.SMEM),
            pl.BlockSpec((1, 1, tm), lambda i, j, be, nv, nu: (jnp.minimum(i + 1, n_blocks - 1), 0, 0),
                         memory_space=pltpu.SMEM),
            pl.BlockSpec((1, 1, tm), lambda i, j, be, nv, nu: (i, 0, 0), memory_space=pltpu.SMEM),
            pl.BlockSpec(memory_space=pl.ANY),
            pl.BlockSpec((1, d), lambda i, j, be, nv, nu: (0, 0)),
            pl.BlockSpec((1, 1, d, tf), lambda i, j, be, nv, nu: (moe_idx, be[i], 0, j_eff(i, j, nu))),
            pl.BlockSpec((1, 1, d, tf), lambda i, j, be, nv, nu: (moe_idx, be[i], 0, j_eff(i, j, nu))),
            pl.BlockSpec((1, 1, tf, d), lambda i, j, be, nv, nu: (moe_idx, be[i], j_eff(i, j, nu), 0)),
        ],
        out_specs=pl.BlockSpec(memory_space=pl.ANY),
        scratch_shapes=[
            pltpu.VMEM((2, tm // 8, 8, d), F32),
            pltpu.VMEM((tm, d), BF16),
            pltpu.VMEM((tm, d), F32),
            pltpu.VMEM((tm // 8, 8, d), F32),
            pltpu.SemaphoreType.DMA((2,)),
            pltpu.SemaphoreType.DMA(()),
        ],
    )
    return pl.pallas_call(
        functools.partial(_moe_kernel, tm=tm),
        grid_spec=grid_spec,
        out_shape=jax.ShapeDtypeStruct((TOP_K * n, d), F32),
        compiler_params=_cparams(("arbitrary", "arbitrary")),
        name="moe_experts",
    )(block_expert, n_valid, n_used, tok, tok, dst, x, norm_w.reshape(1, d), wg, wu, wd)


def _combine_kernel(x_ref, y0_ref, y1_ref, g_ref, o_ref):
    g = g_ref[...]
    o_ref[...] = x_ref[...] + g[:, 0:1] * y0_ref[0] + g[:, 1:2] * y1_ref[0]


def _combine_norm_kernel(x_ref, y0_ref, y1_ref, g_ref, w_ref, o_ref):
    g = g_ref[...]
    o_ref[...] = _rmsnorm_rows(x_ref[...] + g[:, 0:1] * y0_ref[0] + g[:, 1:2] * y1_ref[0], w_ref[...])


def moe_combine(x, y, gates, out_norm_w=None, tm=512):
    n, d = x.shape
    in_specs = [pl.BlockSpec((tm, d), lambda i: (i, 0)),
                pl.BlockSpec((1, tm, d), lambda i: (0, i, 0)),
                pl.BlockSpec((1, tm, d), lambda i: (1, i, 0)),
                pl.BlockSpec((tm, TOP_K), lambda i: (i, 0))]
    args = [x, y, y, gates]
    body = _combine_kernel
    if out_norm_w is not None:
        in_specs.append(pl.BlockSpec((1, d), lambda i: (0, 0)))
        args.append(out_norm_w.reshape(1, d))
        body = _combine_norm_kernel
    return pl.pallas_call(
        body,
        grid=(n // tm,),
        in_specs=in_specs,
        out_specs=pl.BlockSpec((tm, d), lambda i: (i, 0)),
        out_shape=jax.ShapeDtypeStruct((n, d), F32),
        compiler_params=_cparams(("parallel",)),
        name="moe_combine",
    )(*args)


def moe_layer(x, norm_w, router, wg, wu, wd, moe_idx, out_norm_w=None, tm=1024):
    n, d = x.shape
    n_assign = n * TOP_K
    top_idx, gates = router_top2(x, norm_w, router)

    flat_e = top_idx.reshape(-1)
    order = jnp.argsort(flat_e).astype(jnp.int32)
    counts = jnp.sum((flat_e[:, None] == jnp.arange(N_EXPERTS, dtype=jnp.int32)[None, :]).astype(jnp.int32), axis=0)
    starts = jnp.cumsum(counts) - counts
    n_blk_e = (counts + tm - 1) // tm
    blk_ends = jnp.cumsum(n_blk_e)
    blk_starts = blk_ends - n_blk_e
    n_blocks = n_assign // tm + N_EXPERTS
    blk = jnp.arange(n_blocks, dtype=jnp.int32)
    block_expert = jnp.minimum(jnp.sum((blk[:, None] >= blk_ends[None, :]).astype(jnp.int32), axis=1),
                               N_EXPERTS - 1)
    n_used = blk_ends[-1].astype(jnp.int32).reshape(1)
    row0 = (blk - blk_starts[block_expert]) * tm
    n_valid = jnp.where(blk < n_used[0], jnp.clip(counts[block_expert] - row0, 0, tm), 0).astype(jnp.int32)
    pos = starts[block_expert][:, None] + row0[:, None] + jnp.arange(tm, dtype=jnp.int32)[None, :]
    asg = jnp.take(order, jnp.clip(pos, 0, n_assign - 1), axis=0).reshape(n_blocks, 1, tm)
    tok = lax.shift_right_logical(asg, 1)
    dst = (asg & 1) * n + tok

    y = moe_experts(x, norm_w, tok, dst, block_expert, n_valid, n_used, wg, wu, wd, moe_idx, tm)
    return moe_combine(x, y.reshape(TOP_K, n, d), gates, out_norm_w)


def kernel(x, mix_norm, ffn_norm, hgrn_w_in, hgrn_lb_raw, hgrn_gnorm, hgrn_w_out, kv_norm, w_kv, sb_w_q, sb_w_o, ffn_w_gu, ffn_w_down, moe_router, moe_w_gate, moe_w_up, moe_w_down, final_norm):
    bsz, seq, d = x.shape
    n = bsz * seq
    depth = mix_norm.shape[0]
    n_a = hgrn_w_in.shape[0]
    assert depth % 2 == 0, "the final rmsnorm is fused into the last (expert) layer's combine"
    xs = x.reshape(n, d)

    p = jax.nn.softmax(hgrn_lb_raw.astype(F32), axis=0)
    lower_bounds = jnp.cumsum(p, axis=0) - p[0:1]

    kv = None
    for layer in range(depth):
        if layer < n_a:
            qig, f_raw = hgrn_in_proj(xs, mix_norm[layer], hgrn_w_in, layer)
            og = hgrn_recurrence(qig, f_raw, lower_bounds[layer], hgrn_gnorm[layer], bsz, seq)
            xs = matmul_residual(og, hgrn_w_out, layer, xs)
        else:
            j = layer - n_a
            q = norm_matmul(xs, mix_norm[layer], sb_w_q, j, BF16, out_scale=SB_LOGIT_SCALE_LOG2)
            att = stick_breaking(q, kv, bsz, seq)
            xs = matmul_residual(att, sb_w_o, j, xs)
        if layer % 2 == 0:
            xs = swiglu_residual(xs, ffn_norm[layer], ffn_w_gu[layer // 2].astype(BF16),
                                 ffn_w_down[layer // 2].astype(BF16))
        else:
            e = layer // 2
            xs = moe_layer(xs, ffn_norm[layer], moe_router[e], moe_w_gate, moe_w_up, moe_w_down, e,
                           out_norm_w=final_norm if layer == depth - 1 else None)
        if layer == n_a - 1:
            kv = norm_matmul(xs, kv_norm, w_kv.reshape(1, d, 2 * d), 0, BF16)
    return xs.reshape(bsz, seq, d)
```

```python
import functools

import jax
import jax.numpy as jnp
from jax import lax
from jax.experimental import pallas as pl
from jax.experimental.pallas import tpu as pltpu

F32 = jnp.float32
BF16 = jnp.bfloat16

D_MODEL = 1024
HGRN_HEADS = 8
HGRN_DK = 128
HGRN_CHUNK = 64
SB_HEADS = 16
SB_HEAD_DIM = 64
N_EXPERTS = 8
TOP_K = 2
EPS = 1e-6
F_FLOOR = 1e-30
LOG2E = 1.4426950408889634

VMEM_LIMIT_BYTES = 52 * 1024 * 1024
EXP2_ZERO_BELOW = -150.0
HGRN_MAX_FACTORED_SPAN = 100.0
SB_LOGIT_SCALE_LOG2 = (SB_HEAD_DIM ** -0.5) * LOG2E


def _cparams(sem):
    return pltpu.CompilerParams(dimension_semantics=sem, vmem_limit_bytes=VMEM_LIMIT_BYTES)


def _sigmoid(x):
    return 0.5 * jnp.tanh(0.5 * x) + 0.5


def _rmsnorm_rows(x, w):
    ms = jnp.mean(x * x, axis=-1, keepdims=True)
    return x * lax.rsqrt(ms + EPS) * w


def _dot_nt(a, b):
    return lax.dot_general(a, b, (((1,), (1,)), ((), ())), preferred_element_type=F32)


def _dot_tn(a, b):
    return lax.dot_general(a, b, (((0,), (0,)), ((), ())), preferred_element_type=F32)


def _norm_matmul_kernel(x_ref, nw_ref, w_ref, o_ref, h_ref, *, out_scale):
    @pl.when(pl.program_id(1) == 0)
    def _():
        h_ref[...] = _rmsnorm_rows(x_ref[...], nw_ref[...]).astype(BF16)

    acc = jnp.dot(h_ref[...], w_ref[0].astype(BF16), preferred_element_type=F32)
    if out_scale is not None:
        acc = acc * out_scale
    o_ref[...] = acc.astype(o_ref.dtype)


def norm_matmul(x, norm_w, w_stack, layer, out_dtype, out_scale=None, tm=1024, tn=1024):
    n, d = x.shape
    n_out = w_stack.shape[2]
    return pl.pallas_call(
        functools.partial(_norm_matmul_kernel, out_scale=out_scale),
        grid=(n // tm, n_out // tn),
        in_specs=[
            pl.BlockSpec((tm, d), lambda i, j: (i, 0)),
            pl.BlockSpec((1, d), lambda i, j: (0, 0)),
            pl.BlockSpec((1, d, tn), lambda i, j: (layer, 0, j)),
        ],
        out_specs=pl.BlockSpec((tm, tn), lambda i, j: (i, j)),
        out_shape=jax.ShapeDtypeStruct((n, n_out), out_dtype),
        scratch_shapes=[pltpu.VMEM((tm, d), BF16)],
        compiler_params=_cparams(("parallel", "arbitrary")),
        name="norm_matmul",
    )(x, norm_w.reshape(1, d), w_stack)


def _matmul_res_kernel(a_ref, w_ref, r_ref, o_ref):
    o_ref[...] = r_ref[...] + jnp.dot(a_ref[...], w_ref[0].astype(BF16), preferred_element_type=F32)


def matmul_residual(a_bf16, w_stack, layer, res, tm=1024):
    n, k = a_bf16.shape
    n_out = w_stack.shape[2]
    return pl.pallas_call(
        _matmul_res_kernel,
        grid=(n // tm,),
        in_specs=[
            pl.BlockSpec((tm, k), lambda i: (i, 0)),
            pl.BlockSpec((1, k, n_out), lambda i: (layer, 0, 0)),
            pl.BlockSpec((tm, n_out), lambda i: (i, 0)),
        ],
        out_specs=pl.BlockSpec((tm, n_out), lambda i: (i, 0)),
        out_shape=jax.ShapeDtypeStruct((n, n_out), F32),
        compiler_params=_cparams(("parallel",)),
        name="matmul_residual",
    )(a_bf16, w_stack, res)


def _cumsum_rows(x, row):
    n = x.shape[0]
    s = 1
    while s < n:
        x = x + jnp.where(row >= s, pltpu.roll(x, s, axis=0), 0.0)
        s *= 2
    return x


def _hgrn_finish(o, gate_raw, gw):
    o = o * lax.rsqrt(jnp.mean(o * o, axis=-1, keepdims=True) + EPS) * gw
    return o * (gate_raw * _sigmoid(gate_raw))


def _hgrn_exact_head(q, k, v, g, state_t):
    c = HGRN_CHUNK
    qg = (q * jnp.exp2(g)).astype(BF16)
    o_inter = _dot_nt(qg, state_t.astype(BF16))
    nv = c // 8
    g8 = [g[8 * j:8 * (j + 1), :] for j in range(nv)]
    q8 = [q[8 * j:8 * (j + 1), :] for j in range(nv)]
    acc8 = [o_inter[8 * j:8 * (j + 1), :] for j in range(nv)]
    row8 = lax.broadcasted_iota(jnp.int32, (8, HGRN_DK), 0)
    for s in range(c):
        j0, r = divmod(s, 8)
        gs = g8[j0][r:r + 1, :]
        ks = k[s:s + 1, :]
        vs = v[s:s + 1, :]
        for j in range(j0, nv):
            diff = g8[j] - gs
            if j == j0 and r > 0:
                m = row8 >= r
                p = jnp.where(m, q8[j] * jnp.exp2(jnp.where(m, diff, 0.0)) * ks, 0.0)
            else:
                p = q8[j] * jnp.exp2(diff) * ks
            cs = jnp.sum(p, axis=-1, keepdims=True)
            acc8[j] = acc8[j] + cs * vs
    o = jnp.concatenate(acc8, axis=0)
    g_last = g[c - 1:c, :]
    kd = (k * jnp.exp2(g_last - g)).astype(BF16)
    new_state_t = jnp.exp2(g_last) * state_t + _dot_tn(v.astype(BF16), kd)
    return o, new_state_t


def _hgrn_kernel(q_ref, f_ref, i_ref, g_ref, lb_ref, gw_ref, o_ref, state_ref, qs_ref, ks_ref, gs_ref,
                 *, n_chunks):
    @pl.when(pl.program_id(1) == 0)
    def _():
        state_ref[...] = jnp.zeros_like(state_ref)

    c = HGRN_CHUNK
    half = c // 2
    dk = HGRN_DK
    lb = lb_ref[...]
    gw = gw_ref[...]
    row = lax.broadcasted_iota(jnp.int32, (c, D_MODEL), 0)
    t_idx = lax.broadcasted_iota(jnp.int32, (c, 2 * c), 0)
    col = lax.broadcasted_iota(jnp.int32, (c, 2 * c), 1)
    upper = jnp.where(t_idx >= half, t_idx, t_idx + c)
    lower = jnp.where(t_idx >= half, 0, c)
    score_mask = jnp.logical_and(col >= lower, col <= upper)
    pad_rows = jnp.zeros((half, D_MODEL), BF16)

    def body(ci, carry):
        rows = pl.ds(pl.multiple_of(ci * c, c), c)
        qr = q_ref[rows, :].astype(F32)
        fr = f_ref[rows, :]
        q = qr * _sigmoid(qr)
        sig = _sigmoid(fr)
        f = lb + (1.0 - lb) * sig
        k = (1.0 - lb) * (1.0 - sig)
        g = _cumsum_rows(jnp.log2(jnp.maximum(f, F_FLOOR)), row)
        qs_ref[...] = q
        ks_ref[...] = k
        gs_ref[...] = g
        g_mid = g[half - 1:half, :]
        g_last = g[c - 1:c, :]
        span = jnp.maximum(jnp.max(-g_mid), jnp.max(g_mid - g_last))

        @pl.when(span <= HGRN_MAX_FACTORED_SPAN)
        def _():
            q = qs_ref[...]
            k = ks_ref[...]
            g = gs_ref[...]
            qg = (q * jnp.exp2(g)).astype(BF16)
            qb = (q[half:, :] * jnp.exp2(g[half:, :] - g_mid)).astype(BF16)
            kb_f32 = k * jnp.exp2(g_mid - g)
            ka = (kb_f32[:half, :] * jnp.exp2(-g_mid)).astype(BF16)
            kd = (kb_f32 * jnp.exp2(g_last - g_mid)).astype(BF16)
            dl = jnp.exp2(g_last)
            vb = i_ref[rows, :]
            gate_raw = g_ref[rows, :].astype(F32)
            q_mix = jnp.concatenate([qg[:half, :], qb], axis=0)
            k_stack = jnp.concatenate([kb_f32.astype(BF16), ka, pad_rows], axis=0)
            v_stack = jnp.concatenate([vb, vb[:half, :], pad_rows], axis=0)
            heads = [slice(h * dk, (h + 1) * dk) for h in range(HGRN_HEADS)]
            states = [state_ref[h] for h in range(HGRN_HEADS)]
            scores = [_dot_nt(q_mix[:, sl], k_stack[:, sl]) for sl in heads]
            inter = [_dot_nt(qg[:, sl], st.astype(BF16)) for sl, st in zip(heads, states)]
            upd = [_dot_tn(vb[:, sl], kd[:, sl]) for sl in heads]
            scores = [jnp.where(score_mask, s, 0.0).astype(BF16) for s in scores]
            outs = [oi + jnp.dot(s, v_stack[:, sl], preferred_element_type=F32)
                    for oi, s, sl in zip(inter, scores, heads)]
            for h, sl in enumerate(heads):
                state_ref[h] = dl[:, sl] * states[h] + upd[h]
                o_ref[rows, sl] = _hgrn_finish(outs[h], gate_raw[:, sl], gw).astype(o_ref.dtype)

        @pl.when(jnp.logical_not(span <= HGRN_MAX_FACTORED_SPAN))
        def _():
            def head_body(h, hc):
                cols = pl.ds(pl.multiple_of(h * dk, dk), dk)
                o, new_state = _hgrn_exact_head(qs_ref[:, cols], ks_ref[:, cols], i_ref[rows, cols].astype(F32),
                                                gs_ref[:, cols], state_ref[h])
                state_ref[h] = new_state
                o_ref[rows, cols] = _hgrn_finish(o, g_ref[rows, cols].astype(F32), gw).astype(o_ref.dtype)
                return hc

            lax.fori_loop(0, HGRN_HEADS, head_body, 0)

        return carry

    lax.fori_loop(0, n_chunks, body, 0)


def _hgrn_in_proj_kernel(x_ref, nw_ref, w_ref, qig_ref, f_ref, h_ref):
    j = pl.program_id(1)

    @pl.when(j == 0)
    def _():
        h_ref[...] = _rmsnorm_rows(x_ref[...], nw_ref[...]).astype(BF16)

    acc = jnp.dot(h_ref[...], w_ref[0].astype(BF16), preferred_element_type=F32)

    @pl.when(j == 1)
    def _():
        f_ref[...] = acc

    @pl.when(j != 1)
    def _():
        qig_ref[...] = acc.astype(qig_ref.dtype)


def hgrn_in_proj(x, norm_w, w_stack, layer, tm=1024):
    n, d = x.shape
    assert w_stack.shape[2] == 4 * d
    return pl.pallas_call(
        _hgrn_in_proj_kernel,
        grid=(n // tm, 4),
        in_specs=[
            pl.BlockSpec((tm, d), lambda i, j: (i, 0)),
            pl.BlockSpec((1, d), lambda i, j: (0, 0)),
            pl.BlockSpec((1, d, d), lambda i, j: (layer, 0, j)),
        ],
        out_specs=[pl.BlockSpec((tm, d), lambda i, j: (i, jnp.maximum(j - 1, 0))),
                   pl.BlockSpec((tm, d), lambda i, j: (i, 0))],
        out_shape=[jax.ShapeDtypeStruct((n, 3 * d), BF16), jax.ShapeDtypeStruct((n, d), F32)],
        scratch_shapes=[pltpu.VMEM((tm, d), BF16)],
        compiler_params=_cparams(("parallel", "arbitrary")),
        name="hgrn_in_proj",
    )(x, norm_w.reshape(1, d), w_stack)


def hgrn_recurrence(qig, f_raw, lb, gnorm_w, bsz, seq, block_len=512):
    n = bsz * seq
    nl = seq // block_len
    d = D_MODEL

    def col_spec(part):
        return pl.BlockSpec((block_len, d), lambda b, l: (b * nl + l, part))

    return pl.pallas_call(
        functools.partial(_hgrn_kernel, n_chunks=block_len // HGRN_CHUNK),
        grid=(bsz, nl),
        in_specs=[col_spec(0), col_spec(0), col_spec(1), col_spec(2),
                  pl.BlockSpec((1, d), lambda b, l: (0, 0)),
                  pl.BlockSpec((1, HGRN_DK), lambda b, l: (0, 0))],
        out_specs=pl.BlockSpec((block_len, d), lambda b, l: (b * nl + l, 0)),
        out_shape=jax.ShapeDtypeStruct((n, d), BF16),
        scratch_shapes=[pltpu.VMEM((HGRN_HEADS, HGRN_DK, HGRN_DK), F32),
                        pltpu.VMEM((HGRN_CHUNK, d), F32),
                        pltpu.VMEM((HGRN_CHUNK, d), F32),
                        pltpu.VMEM((HGRN_CHUNK, d), F32)],
        compiler_params=_cparams(("parallel", "arbitrary")),
        name="hgrn_recurrence",
    )(qig, f_raw, qig, qig, lb.reshape(1, d), gnorm_w.reshape(1, HGRN_DK))


def _sb_kernel(q_ref, k_ref, v_ref, o_ref, acc_ref, rem_ref, *, tq, n_pairs):
    i = pl.program_id(2)
    pw = 2 * SB_HEAD_DIM
    n_heads = 2 * n_pairs
    first_head = lax.broadcasted_iota(jnp.int32, (tq, pw), 1) < SB_HEAD_DIM
    q_pairs = []
    for p in range(n_pairs):
        q = q_ref[:, p * pw:(p + 1) * pw]
        zero = jnp.zeros_like(q)
        q_pairs.append(jnp.concatenate([jnp.where(first_head, q, zero), jnp.where(first_head, zero, q)], axis=0))
    diag_mask = (lax.broadcasted_iota(jnp.int32, (n_heads * tq, tq), 1)
                 < lax.broadcasted_iota(jnp.int32, (n_heads * tq, tq), 0) % tq)
    later = (lax.broadcasted_iota(jnp.int32, (tq, tq), 0)
             > lax.broadcasted_iota(jnp.int32, (tq, tq), 1)).astype(BF16)

    def sweep_block(kb, masked):
        k0 = pl.multiple_of(kb * tq, tq)
        z = jnp.concatenate(
            [_dot_nt(q_pairs[p], k_ref[pl.ds(k0, tq), p * pw:(p + 1) * pw]) for p in range(n_pairs)],
            axis=0)
        soft = jnp.log2(1.0 + jnp.exp2(jnp.minimum(z, -z)))
        log_beta = jnp.minimum(z, 0.0) - soft
        log_rem = log_beta - z
        if masked:
            log_rem = jnp.where(diag_mask, log_rem, 0.0)
        between = jnp.dot(log_rem.astype(BF16), later, preferred_element_type=F32)
        if masked:
            w = jnp.exp2(jnp.where(diag_mask, log_beta + between, -jnp.inf)).astype(BF16)
            rem = jnp.broadcast_to(jnp.sum(log_rem, axis=-1, keepdims=True), rem_ref.shape)
        else:
            rem = rem_ref[...]
            w = jnp.exp2(log_beta + between + jnp.concatenate([rem] * (tq // rem.shape[1]), axis=1)).astype(BF16)
            rem = rem + jnp.sum(log_rem, axis=-1, keepdims=True)
        rem_ref[...] = rem
        for p in range(n_pairs):
            rows = slice(2 * p * tq, 2 * (p + 1) * tq)
            pv = jnp.dot(w[rows, :], v_ref[pl.ds(k0, tq), p * pw:(p + 1) * pw], preferred_element_type=F32)
            if masked:
                acc_ref[rows, :] = pv
            else:
                acc_ref[rows, :] += pv
        return jnp.max(rem)

    def cond(carry):
        kb, rem_max = carry
        return jnp.logical_and(kb >= 0, rem_max > EXP2_ZERO_BELOW)

    def body(carry):
        kb, _ = carry
        return kb - 1, sweep_block(kb, masked=False)

    lax.while_loop(cond, body, (i - 1, sweep_block(i, masked=True)))
    for p in range(n_pairs):
        first = acc_ref[2 * p * tq:(2 * p + 1) * tq, :]
        second = acc_ref[(2 * p + 1) * tq:(2 * p + 2) * tq, :]
        o_ref[:, p * pw:(p + 1) * pw] = jnp.where(first_head, first, second).astype(o_ref.dtype)


def stick_breaking(q, kv, bsz, seq, tq=256, n_pairs=2):
    n = bsz * seq
    nq = seq // tq
    groups = SB_HEADS // (2 * n_pairs)
    w = 2 * SB_HEAD_DIM * n_pairs
    return pl.pallas_call(
        functools.partial(_sb_kernel, tq=tq, n_pairs=n_pairs),
        grid=(bsz, groups, nq),
        in_specs=[
            pl.BlockSpec((tq, w), lambda b, p, i: (b * nq + i, p)),
            pl.BlockSpec((seq, w), lambda b, p, i: (b, p)),
            pl.BlockSpec((seq, w), lambda b, p, i: (b, groups + p)),
        ],
        out_specs=pl.BlockSpec((tq, w), lambda b, p, i: (b * nq + i, p)),
        out_shape=jax.ShapeDtypeStruct((n, D_MODEL), BF16),
        scratch_shapes=[pltpu.VMEM((2 * n_pairs * tq, 2 * SB_HEAD_DIM), F32),
                        pltpu.VMEM((2 * n_pairs * tq, 2 * SB_HEAD_DIM), F32)],
        compiler_params=_cparams(("parallel", "parallel", "arbitrary")),
        name="stick_breaking",
    )(q, kv, kv)


def _swiglu_kernel(x_ref, nw_ref, wg_ref, wu_ref, wd_ref, o_ref, h_ref):
    @pl.when(pl.program_id(1) == 0)
    def _():
        x = x_ref[...]
        h_ref[...] = _rmsnorm_rows(x, nw_ref[...]).astype(BF16)
        o_ref[...] = x

    h = h_ref[...]
    gate = jnp.dot(h, wg_ref[...], preferred_element_type=F32)
    up = jnp.dot(h, wu_ref[...], preferred_element_type=F32)
    act = (gate * _sigmoid(gate) * up).astype(BF16)
    o_ref[...] += jnp.dot(act, wd_ref[...], preferred_element_type=F32)


def swiglu_residual(x, norm_w, w_gu_bf16, w_down_bf16, tm=1024, tf=1408):
    n, d = x.shape
    f = w_down_bf16.shape[0]
    nf = f // tf
    return pl.pallas_call(
        _swiglu_kernel,
        grid=(n // tm, nf),
        in_specs=[
            pl.BlockSpec((tm, d), lambda i, j: (i, 0)),
            pl.BlockSpec((1, d), lambda i, j: (0, 0)),
            pl.BlockSpec((d, tf), lambda i, j: (0, j)),
            pl.BlockSpec((d, tf), lambda i, j: (0, nf + j)),
            pl.BlockSpec((tf, d), lambda i, j: (j, 0)),
        ],
        out_specs=pl.BlockSpec((tm, d), lambda i, j: (i, 0)),
        out_shape=jax.ShapeDtypeStruct((n, d), F32),
        scratch_shapes=[pltpu.VMEM((tm, d), BF16)],
        compiler_params=_cparams(("parallel", "arbitrary")),
        name="swiglu_residual",
    )(x, norm_w.reshape(1, d), w_gu_bf16, w_gu_bf16, w_down_bf16)


ROUTER_LANES = 128


def _split_bf16(a):
    hi = a.astype(BF16)
    return hi, (a - hi.astype(F32)).astype(BF16)


def _router_kernel(x_ref, nw_ref, r_ref, idx_ref, gate_ref):
    h_hi, h_lo = _split_bf16(_rmsnorm_rows(x_ref[...], nw_ref[...]))
    r_hi, r_lo = _split_bf16(r_ref[...])
    logits = (jnp.dot(h_hi, r_hi, preferred_element_type=F32) + jnp.dot(h_hi, r_lo, preferred_element_type=F32)
              + jnp.dot(h_lo, r_hi, preferred_element_type=F32))
    lane = lax.broadcasted_iota(jnp.int32, logits.shape, 1)
    lane_f = lane.astype(F32)
    logits = jnp.where(lane < N_EXPERTS, logits, -jnp.inf)
    m1 = jnp.max(logits, axis=-1, keepdims=True)
    i1 = jnp.min(jnp.where(logits == m1, lane_f, float(ROUTER_LANES)), axis=-1, keepdims=True)
    rest = jnp.where(lane_f == i1, -jnp.inf, logits)
    m2 = jnp.max(rest, axis=-1, keepdims=True)
    i2 = jnp.min(jnp.where(rest == m2, lane_f, float(ROUTER_LANES)), axis=-1, keepdims=True)
    e = jnp.exp(m2 - m1)
    g1 = 1.0 / (1.0 + e)
    first = lane == 0
    idx_ref[...] = jnp.where(first, i1, i2)[:, :TOP_K].astype(jnp.int32)
    gate_ref[...] = jnp.where(first, g1, e * g1)[:, :TOP_K]


def router_top2(x, norm_w, router, tm=1024):
    n, d = x.shape
    r_pad = jnp.zeros((d, ROUTER_LANES), F32).at[:, :N_EXPERTS].set(router)
    return pl.pallas_call(
        _router_kernel,
        grid=(n // tm,),
        in_specs=[
            pl.BlockSpec((tm, d), lambda i: (i, 0)),
            pl.BlockSpec((1, d), lambda i: (0, 0)),
            pl.BlockSpec((d, ROUTER_LANES), lambda i: (0, 0)),
        ],
        out_specs=[pl.BlockSpec((tm, TOP_K), lambda i: (i, 0)), pl.BlockSpec((tm, TOP_K), lambda i: (i, 0))],
        out_shape=[jax.ShapeDtypeStruct((n, TOP_K), jnp.int32), jax.ShapeDtypeStruct((n, TOP_K), F32)],
        compiler_params=_cparams(("parallel",)),
        name="moe_router",
    )(x, norm_w.reshape(1, d), r_pad)


def _moe_kernel(be_ref, nv_ref, nu_ref, tok_ref, tok_next_ref, dst_ref, x_hbm, nw_ref, wg_ref, wu_ref, wd_ref,
                out_hbm, xbuf, hbuf, acc_ref, ybuf, gsem, ssem, *, tm):
    i = pl.program_id(0)
    j = pl.program_id(1)
    last_j = pl.num_programs(1) - 1
    n_used = nu_ref[0]
    used = i < n_used

    def gather_copy(tok, slot, g, u):
        return pltpu.make_async_copy(x_hbm.at[pl.ds(tok, 1), :], xbuf.at[slot, g, pl.ds(u, 1), :], gsem.at[slot])

    def scatter_copy(dst, g, u):
        return pltpu.make_async_copy(ybuf.at[g, pl.ds(u, 1), :], out_hbm.at[pl.ds(dst, 1), :], ssem)

    def for_rows(count, issue):
        n_groups = lax.shift_right_logical(count, 3)

        def group(g, c):
            for u in range(8):
                issue(g * 8 + u, g, u)
            return c

        def single(r, c):
            issue(r, n_groups, r - n_groups * 8)
            return c

        lax.fori_loop(0, n_groups, group, 0)
        lax.fori_loop(n_groups * 8, count, single, 0)

    def start_gather(idx_ref, slot, count):
        for_rows(count, lambda r, g, u: gather_copy(idx_ref[0, 0, r], slot, g, u).start())

    def wait_rows(buf, sem, count):
        p = tm
        while p >= 1:
            part = buf.at[pl.ds(0, p // 8)] if p >= 8 else buf.at[0, pl.ds(0, p), :]

            @pl.when((count & p) != 0)
            def _(part=part):
                pltpu.make_async_copy(part, part, sem).wait()

            p //= 2

    @pl.when(jnp.logical_and(used, j == 0))
    def _():
        slot = lax.rem(i, 2)

        @pl.when(i == 0)
        def _():
            xbuf[...] = jnp.zeros_like(xbuf)
            start_gather(tok_ref, 0, nv_ref[0])

        wait_rows(xbuf.at[slot], gsem.at[slot], nv_ref[i])

        @pl.when(i + 1 < n_used)
        def _():
            start_gather(tok_next_ref, 1 - slot, nv_ref[i + 1])

        x = xbuf[slot].reshape(tm, xbuf.shape[-1])
        hbuf[...] = _rmsnorm_rows(x, nw_ref[...]).astype(BF16)

    @pl.when(used)
    def _():
        h = hbuf[...]
        gate = jnp.dot(h, wg_ref[0, 0].astype(BF16), preferred_element_type=F32)
        up = jnp.dot(h, wu_ref[0, 0].astype(BF16), preferred_element_type=F32)
        act = (gate * _sigmoid(gate) * up).astype(BF16)
        part = jnp.dot(act, wd_ref[0, 0].astype(BF16), preferred_element_type=F32)

        @pl.when(j == 0)
        def _():
            acc_ref[...] = part

        @pl.when(jnp.logical_and(j > 0, j < last_j))
        def _():
            acc_ref[...] += part

        @pl.when(j == last_j)
        def _():
            @pl.when(i > 0)
            def _():
                wait_rows(ybuf, ssem, nv_ref[i - 1])

            ybuf[...] = (acc_ref[...] + part).reshape(ybuf.shape)

            for_rows(nv_ref[i], lambda r, g, u: scatter_copy(dst_ref[0, 0, r], g, u).start())

            @pl.when(i == n_used - 1)
            def _():
                wait_rows(ybuf, ssem, nv_ref[i])


def moe_experts(x, norm_w, tok, dst, block_expert, n_valid, n_used, wg, wu, wd, moe_idx, tm, tf=512):
    n, d = x.shape
    f = wg.shape[3]
    n_blocks = tok.shape[0]
    nj = f // tf

    def j_eff(i, j, nu):
        return jnp.where(i < nu[0], j, nj - 1)

    grid_spec = pltpu.PrefetchScalarGridSpec(
        num_scalar_prefetch=3,
        grid=(n_blocks, nj),
        in_specs=[
            pl.BlockSpec((1, 1, tm), lambda i, j, be, nv, nu: (i, 0, 0), memory_space=pltpu.SMEM),
            pl.BlockSpec((1, 1, tm), lambda i, j, be, nv, nu: (jnp.minimum(i + 1, n_blocks - 1), 0, 0),
                         memory_space=pltpu.SMEM),
            pl.BlockSpec((1, 1, tm), lambda i, j, be, nv, nu: (i, 0, 0), memory_space=pltpu.SMEM),
            pl.BlockSpec(memory_space=pl.ANY),
            pl.BlockSpec((1, d), lambda i, j, be, nv, nu: (0, 0)),
            pl.BlockSpec((1, 1, d, tf), lambda i, j, be, nv, nu: (moe_idx, be[i], 0, j_eff(i, j, nu))),
            pl.BlockSpec((1, 1, d, tf), lambda i, j, be, nv, nu: (moe_idx, be[i], 0, j_eff(i, j, nu))),
            pl.BlockSpec((1, 1, tf, d), lambda i, j, be, nv, nu: (moe_idx, be[i], j_eff(i, j, nu), 0)),
        ],
        out_specs=pl.BlockSpec(memory_space=pl.ANY),
        scratch_shapes=[
            pltpu.VMEM((2, tm // 8, 8, d), F32),
            pltpu.VMEM((tm, d), BF16),
            pltpu.VMEM((tm, d), F32),
            pltpu.VMEM((tm // 8, 8, d), F32),
            pltpu.SemaphoreType.DMA((2,)),
            pltpu.SemaphoreType.DMA(()),
        ],
    )
    return pl.pallas_call(
        functools.partial(_moe_kernel, tm=tm),
        grid_spec=grid_spec,
        out_shape=jax.ShapeDtypeStruct((TOP_K * n, d), F32),
        compiler_params=_cparams(("arbitrary", "arbitrary")),
        name="moe_experts",
    )(block_expert, n_valid, n_used, tok, tok, dst, x, norm_w.reshape(1, d), wg, wu, wd)


def _combine_kernel(x_ref, y0_ref, y1_ref, g_ref, o_ref):
    g = g_ref[...]
    o_ref[...] = x_ref[...] + g[:, 0:1] * y0_ref[0] + g[:, 1:2] * y1_ref[0]


def _combine_norm_kernel(x_ref, y0_ref, y1_ref, g_ref, w_ref, o_ref):
    g = g_ref[...]
    o_ref[...] = _rmsnorm_rows(x_ref[...] + g[:, 0:1] * y0_ref[0] + g[:, 1:2] * y1_ref[0], w_ref[...])


def moe_combine(x, y, gates, out_norm_w=None, tm=512):
    n, d = x.shape
    in_specs = [pl.BlockSpec((tm, d), lambda i: (i, 0)),
                pl.BlockSpec((1, tm, d), lambda i: (0, i, 0)),
                pl.BlockSpec((1, tm, d), lambda i: (1, i, 0)),
                pl.BlockSpec((tm, TOP_K), lambda i: (i, 0))]
    args = [x, y, y, gates]
    body = _combine_kernel
    if out_norm_w is not None:
        in_specs.append(pl.BlockSpec((1, d), lambda i: (0, 0)))
        args.append(out_norm_w.reshape(1, d))
        body = _combine_norm_kernel
    return pl.pallas_call(
        body,
        grid=(n // tm,),
        in_specs=in_specs,
        out_specs=pl.BlockSpec((tm, d), lambda i: (i, 0)),
        out_shape=jax.ShapeDtypeStruct((n, d), F32),
        compiler_params=_cparams(("parallel",)),
        name="moe_combine",
    )(*args)


def moe_layer(x, norm_w, router, wg, wu, wd, moe_idx, out_norm_w=None, tm=1024):
    n, d = x.shape
    n_assign = n * TOP_K
    top_idx, gates = router_top2(x, norm_w, router)

    flat_e = top_idx.reshape(-1)
    order = jnp.argsort(flat_e).astype(jnp.int32)
    counts = jnp.sum((flat_e[:, None] == jnp.arange(N_EXPERTS, dtype=jnp.int32)[None, :]).astype(jnp.int32), axis=0)
    starts = jnp.cumsum(counts) - counts
    n_blk_e = (counts + tm - 1) // tm
    blk_ends = jnp.cumsum(n_blk_e)
    blk_starts = blk_ends - n_blk_e
    n_blocks = n_assign // tm + N_EXPERTS
    blk = jnp.arange(n_blocks, dtype=jnp.int32)
    block_expert = jnp.minimum(jnp.sum((blk[:, None] >= blk_ends[None, :]).astype(jnp.int32), axis=1),
                               N_EXPERTS - 1)
    n_used = blk_ends[-1].astype(jnp.int32).reshape(1)
    row0 = (blk - blk_starts[block_expert]) * tm
    n_valid = jnp.where(blk < n_used[0], jnp.clip(counts[block_expert] - row0, 0, tm), 0).astype(jnp.int32)
    pos = starts[block_expert][:, None] + row0[:, None] + jnp.arange(tm, dtype=jnp.int32)[None, :]
    asg = jnp.take(order, jnp.clip(pos, 0, n_assign - 1), axis=0).reshape(n_blocks, 1, tm)
    tok = lax.shift_right_logical(asg, 1)
    dst = (asg & 1) * n + tok

    y = moe_experts(x, norm_w, tok, dst, block_expert, n_valid, n_used, wg, wu, wd, moe_idx, tm)
    return moe_combine(x, y.reshape(TOP_K, n, d), gates, out_norm_w)


def kernel(x, mix_norm, ffn_norm, hgrn_w_in, hgrn_lb_raw, hgrn_gnorm, hgrn_w_out, kv_norm, w_kv, sb_w_q, sb_w_o, ffn_w_gu, ffn_w_down, moe_router, moe_w_gate, moe_w_up, moe_w_down, final_norm):
    bsz, seq, d = x.shape
    n = bsz * seq
    depth = mix_norm.shape[0]
    n_a = hgrn_w_in.shape[0]
    assert depth % 2 == 0, "the final rmsnorm is fused into the last (expert) layer's combine"
    xs = x.reshape(n, d)

    p = jax.nn.softmax(hgrn_lb_raw.astype(F32), axis=0)
    lower_bounds = jnp.cumsum(p, axis=0) - p[0:1]

    kv = None
    for layer in range(depth):
        if layer < n_a:
            qig, f_raw = hgrn_in_proj(xs, mix_norm[layer], hgrn_w_in, layer)
            og = hgrn_recurrence(qig, f_raw, lower_bounds[layer], hgrn_gnorm[layer], bsz, seq)
            xs = matmul_residual(og, hgrn_w_out, layer, xs)
        else:
            j = layer - n_a
            q = norm_matmul(xs, mix_norm[layer], sb_w_q, j, BF16, out_scale=SB_LOGIT_SCALE_LOG2)
            att = stick_breaking(q, kv, bsz, seq)
            xs = matmul_residual(att, sb_w_o, j, xs)
        if layer % 2 == 0:
            xs = swiglu_residual(xs, ffn_norm[layer], ffn_w_gu[layer // 2].astype(BF16),
                                 ffn_w_down[layer // 2].astype(BF16))
        else:
            e = layer // 2
            xs = moe_layer(xs, ffn_norm[layer], moe_router[e], moe_w_gate, moe_w_up, moe_w_down, e,
                           out_norm_w=final_norm if layer == depth - 1 else None)
        if layer == n_a - 1:
            kv = norm_matmul(xs, kv_norm, w_kv.reshape(1, d, 2 * d), 0, BF16, tn=2 * d)
    return xs.reshape(bsz, seq, d)
```

```python
import functools

import jax
import jax.numpy as jnp
from jax import lax
from jax.experimental import pallas as pl
from jax.experimental.pallas import tpu as pltpu

F32 = jnp.float32
BF16 = jnp.bfloat16

D_MODEL = 1024
HGRN_HEADS = 8
HGRN_DK = 128
HGRN_CHUNK = 64
SB_HEADS = 16
SB_HEAD_DIM = 64
N_EXPERTS = 8
TOP_K = 2
EPS = 1e-6
F_FLOOR = 1e-30
LOG2E = 1.4426950408889634

VMEM_LIMIT_BYTES = 52 * 1024 * 1024
EXP2_ZERO_BELOW = -150.0
HGRN_MAX_FACTORED_SPAN = 100.0
SB_LOGIT_SCALE_LOG2 = (SB_HEAD_DIM ** -0.5) * LOG2E


def _cparams(sem):
    return pltpu.CompilerParams(dimension_semantics=sem, vmem_limit_bytes=VMEM_LIMIT_BYTES)


def _sigmoid(x):
    return 0.5 * jnp.tanh(0.5 * x) + 0.5


def _rmsnorm_rows(x, w):
    ms = jnp.mean(x * x, axis=-1, keepdims=True)
    return x * lax.rsqrt(ms + EPS) * w


def _dot_nt(a, b):
    return lax.dot_general(a, b, (((1,), (1,)), ((), ())), preferred_element_type=F32)


def _dot_tn(a, b):
    return lax.dot_general(a, b, (((0,), (0,)), ((), ())), preferred_element_type=F32)


def _norm_matmul_kernel(x_ref, nw_ref, w_ref, o_ref, h_ref, *, out_scale):
    @pl.when(pl.program_id(1) == 0)
    def _():
        h_ref[...] = _rmsnorm_rows(x_ref[...], nw_ref[...]).astype(BF16)

    acc = jnp.dot(h_ref[...], w_ref[0].astype(BF16), preferred_element_type=F32)
    if out_scale is not None:
        acc = acc * out_scale
    o_ref[...] = acc.astype(o_ref.dtype)


def norm_matmul(x, norm_w, w_stack, layer, out_dtype, out_scale=None, tm=1024, tn=1024):
    n, d = x.shape
    n_out = w_stack.shape[2]
    return pl.pallas_call(
        functools.partial(_norm_matmul_kernel, out_scale=out_scale),
        grid=(n // tm, n_out // tn),
        in_specs=[
            pl.BlockSpec((tm, d), lambda i, j: (i, 0)),
            pl.BlockSpec((1, d), lambda i, j: (0, 0)),
            pl.BlockSpec((1, d, tn), lambda i, j: (layer, 0, j)),
        ],
        out_specs=pl.BlockSpec((tm, tn), lambda i, j: (i, j)),
        out_shape=jax.ShapeDtypeStruct((n, n_out), out_dtype),
        scratch_shapes=[pltpu.VMEM((tm, d), BF16)],
        compiler_params=_cparams(("parallel", "arbitrary")),
        name="norm_matmul",
    )(x, norm_w.reshape(1, d), w_stack)


def _matmul_res_kernel(a_ref, w_ref, r_ref, o_ref):
    o_ref[...] = r_ref[...] + jnp.dot(a_ref[...], w_ref[0].astype(BF16), preferred_element_type=F32)


def matmul_residual(a_bf16, w_stack, layer, res, tm=1024):
    n, k = a_bf16.shape
    n_out = w_stack.shape[2]
    return pl.pallas_call(
        _matmul_res_kernel,
        grid=(n // tm,),
        in_specs=[
            pl.BlockSpec((tm, k), lambda i: (i, 0)),
            pl.BlockSpec((1, k, n_out), lambda i: (layer, 0, 0)),
            pl.BlockSpec((tm, n_out), lambda i: (i, 0)),
        ],
        out_specs=pl.BlockSpec((tm, n_out), lambda i: (i, 0)),
        out_shape=jax.ShapeDtypeStruct((n, n_out), F32),
        compiler_params=_cparams(("parallel",)),
        name="matmul_residual",
    )(a_bf16, w_stack, res)


def _cumsum_rows(x, row):
    n = x.shape[0]
    s = 1
    while s < n:
        x = x + jnp.where(row >= s, pltpu.roll(x, s, axis=0), 0.0)
        s *= 2
    return x


def _hgrn_finish(o, gate_raw, gw):
    o = o * lax.rsqrt(jnp.mean(o * o, axis=-1, keepdims=True) + EPS) * gw
    return o * (gate_raw * _sigmoid(gate_raw))


def _hgrn_exact_head(q, k, v, g, state_t):
    c = HGRN_CHUNK
    qg = (q * jnp.exp2(g)).astype(BF16)
    o_inter = _dot_nt(qg, state_t.astype(BF16))
    nv = c // 8
    g8 = [g[8 * j:8 * (j + 1), :] for j in range(nv)]
    q8 = [q[8 * j:8 * (j + 1), :] for j in range(nv)]
    acc8 = [o_inter[8 * j:8 * (j + 1), :] for j in range(nv)]
    row8 = lax.broadcasted_iota(jnp.int32, (8, HGRN_DK), 0)
    for s in range(c):
        j0, r = divmod(s, 8)
        gs = g8[j0][r:r + 1, :]
        ks = k[s:s + 1, :]
        vs = v[s:s + 1, :]
        for j in range(j0, nv):
            diff = g8[j] - gs
            if j == j0 and r > 0:
                m = row8 >= r
                p = jnp.where(m, q8[j] * jnp.exp2(jnp.where(m, diff, 0.0)) * ks, 0.0)
            else:
                p = q8[j] * jnp.exp2(diff) * ks
            cs = jnp.sum(p, axis=-1, keepdims=True)
            acc8[j] = acc8[j] + cs * vs
    o = jnp.concatenate(acc8, axis=0)
    g_last = g[c - 1:c, :]
    kd = (k * jnp.exp2(g_last - g)).astype(BF16)
    new_state_t = jnp.exp2(g_last) * state_t + _dot_tn(v.astype(BF16), kd)
    return o, new_state_t


def _hgrn_kernel(q_ref, f_ref, i_ref, g_ref, lb_ref, gw_ref, o_ref, state_ref, qs_ref, ks_ref, gs_ref,
                 *, n_chunks):
    @pl.when(pl.program_id(1) == 0)
    def _():
        state_ref[...] = jnp.zeros_like(state_ref)

    c = HGRN_CHUNK
    half = c // 2
    dk = HGRN_DK
    lb = lb_ref[...]
    gw = gw_ref[...]
    row = lax.broadcasted_iota(jnp.int32, (c, D_MODEL), 0)
    t_idx = lax.broadcasted_iota(jnp.int32, (c, 2 * c), 0)
    col = lax.broadcasted_iota(jnp.int32, (c, 2 * c), 1)
    upper = jnp.where(t_idx >= half, t_idx, t_idx + c)
    lower = jnp.where(t_idx >= half, 0, c)
    score_mask = jnp.logical_and(col >= lower, col <= upper)
    pad_rows = jnp.zeros((half, D_MODEL), BF16)

    def body(ci, carry):
        rows = pl.ds(pl.multiple_of(ci * c, c), c)
        qr = q_ref[rows, :].astype(F32)
        fr = f_ref[rows, :]
        q = qr * _sigmoid(qr)
        sig = _sigmoid(fr)
        f = lb + (1.0 - lb) * sig
        k = (1.0 - lb) * (1.0 - sig)
        g = _cumsum_rows(jnp.log2(jnp.maximum(f, F_FLOOR)), row)
        qs_ref[...] = q
        ks_ref[...] = k
        gs_ref[...] = g
        g_mid = g[half - 1:half, :]
        g_last = g[c - 1:c, :]
        span = jnp.maximum(jnp.max(-g_mid), jnp.max(g_mid - g_last))

        @pl.when(span <= HGRN_MAX_FACTORED_SPAN)
        def _():
            q = qs_ref[...]
            k = ks_ref[...]
            g = gs_ref[...]
            qg = (q * jnp.exp2(g)).astype(BF16)
            qb = (q[half:, :] * jnp.exp2(g[half:, :] - g_mid)).astype(BF16)
            kb_f32 = k * jnp.exp2(g_mid - g)
            ka = (kb_f32[:half, :] * jnp.exp2(-g_mid)).astype(BF16)
            kd = (kb_f32 * jnp.exp2(g_last - g_mid)).astype(BF16)
            dl = jnp.exp2(g_last)
            vb = i_ref[rows, :]
            gate_raw = g_ref[rows, :].astype(F32)
            q_mix = jnp.concatenate([qg[:half, :], qb], axis=0)
            k_stack = jnp.concatenate([kb_f32.astype(BF16), ka, pad_rows], axis=0)
            v_stack = jnp.concatenate([vb, vb[:half, :], pad_rows], axis=0)
            heads = [slice(h * dk, (h + 1) * dk) for h in range(HGRN_HEADS)]
            states = [state_ref[h] for h in range(HGRN_HEADS)]
            scores = [_dot_nt(q_mix[:, sl], k_stack[:, sl]) for sl in heads]
            inter = [_dot_nt(qg[:, sl], st.astype(BF16)) for sl, st in zip(heads, states)]
            upd = [_dot_tn(vb[:, sl], kd[:, sl]) for sl in heads]
            scores = [jnp.where(score_mask, s, 0.0).astype(BF16) for s in scores]
            outs = [oi + jnp.dot(s, v_stack[:, sl], preferred_element_type=F32)
                    for oi, s, sl in zip(inter, scores, heads)]
            for h, sl in enumerate(heads):
                state_ref[h] = dl[:, sl] * states[h] + upd[h]
                o_ref[rows, sl] = _hgrn_finish(outs[h], gate_raw[:, sl], gw).astype(o_ref.dtype)

        @pl.when(jnp.logical_not(span <= HGRN_MAX_FACTORED_SPAN))
        def _():
            def head_body(h, hc):
                cols = pl.ds(pl.multiple_of(h * dk, dk), dk)
                o, new_state = _hgrn_exact_head(qs_ref[:, cols], ks_ref[:, cols], i_ref[rows, cols].astype(F32),
                                                gs_ref[:, cols], state_ref[h])
                state_ref[h] = new_state
                o_ref[rows, cols] = _hgrn_finish(o, g_ref[rows, cols].astype(F32), gw).astype(o_ref.dtype)
                return hc

            lax.fori_loop(0, HGRN_HEADS, head_body, 0)

        return carry

    lax.fori_loop(0, n_chunks, body, 0)


def _hgrn_in_proj_kernel(x_ref, nw_ref, w_ref, qig_ref, f_ref, h_ref):
    j = pl.program_id(1)

    @pl.when(j == 0)
    def _():
        h_ref[...] = _rmsnorm_rows(x_ref[...], nw_ref[...]).astype(BF16)

    acc = jnp.dot(h_ref[...], w_ref[0].astype(BF16), preferred_element_type=F32)

    @pl.when(j == 1)
    def _():
        f_ref[...] = acc

    @pl.when(j != 1)
    def _():
        qig_ref[...] = acc.astype(qig_ref.dtype)


def hgrn_in_proj(x, norm_w, w_stack, layer, tm=1024):
    n, d = x.shape
    assert w_stack.shape[2] == 4 * d
    return pl.pallas_call(
        _hgrn_in_proj_kernel,
        grid=(n // tm, 4),
        in_specs=[
            pl.BlockSpec((tm, d), lambda i, j: (i, 0)),
            pl.BlockSpec((1, d), lambda i, j: (0, 0)),
            pl.BlockSpec((1, d, d), lambda i, j: (layer, 0, j)),
        ],
        out_specs=[pl.BlockSpec((tm, d), lambda i, j: (i, jnp.maximum(j - 1, 0))),
                   pl.BlockSpec((tm, d), lambda i, j: (i, 0))],
        out_shape=[jax.ShapeDtypeStruct((n, 3 * d), BF16), jax.ShapeDtypeStruct((n, d), F32)],
        scratch_shapes=[pltpu.VMEM((tm, d), BF16)],
        compiler_params=_cparams(("parallel", "arbitrary")),
        name="hgrn_in_proj",
    )(x, norm_w.reshape(1, d), w_stack)


def hgrn_recurrence(qig, f_raw, lb, gnorm_w, bsz, seq, block_len=512):
    n = bsz * seq
    nl = seq // block_len
    d = D_MODEL

    def col_spec(part):
        return pl.BlockSpec((block_len, d), lambda b, l: (b * nl + l, part))

    return pl.pallas_call(
        functools.partial(_hgrn_kernel, n_chunks=block_len // HGRN_CHUNK),
        grid=(bsz, nl),
        in_specs=[col_spec(0), col_spec(0), col_spec(1), col_spec(2),
                  pl.BlockSpec((1, d), lambda b, l: (0, 0)),
                  pl.BlockSpec((1, HGRN_DK), lambda b, l: (0, 0))],
        out_specs=pl.BlockSpec((block_len, d), lambda b, l: (b * nl + l, 0)),
        out_shape=jax.ShapeDtypeStruct((n, d), BF16),
        scratch_shapes=[pltpu.VMEM((HGRN_HEADS, HGRN_DK, HGRN_DK), F32),
                        pltpu.VMEM((HGRN_CHUNK, d), F32),
                        pltpu.VMEM((HGRN_CHUNK, d), F32),
                        pltpu.VMEM((HGRN_CHUNK, d), F32)],
        compiler_params=_cparams(("parallel", "arbitrary")),
        name="hgrn_recurrence",
    )(qig, f_raw, qig, qig, lb.reshape(1, d), gnorm_w.reshape(1, HGRN_DK))


def _sb_kernel(q_ref, k_ref, v_ref, o_ref, acc_ref, rem_ref, *, tq, n_pairs):
    i = pl.program_id(2)
    pw = 2 * SB_HEAD_DIM
    n_heads = 2 * n_pairs
    first_head = lax.broadcasted_iota(jnp.int32, (tq, pw), 1) < SB_HEAD_DIM
    q_pairs = []
    for p in range(n_pairs):
        q = q_ref[:, p * pw:(p + 1) * pw]
        zero = jnp.zeros_like(q)
        q_pairs.append(jnp.concatenate([jnp.where(first_head, q, zero), jnp.where(first_head, zero, q)], axis=0))
    diag_mask = (lax.broadcasted_iota(jnp.int32, (n_heads * tq, tq), 1)
                 < lax.broadcasted_iota(jnp.int32, (n_heads * tq, tq), 0) % tq)
    later = (lax.broadcasted_iota(jnp.int32, (tq, tq), 0)
             > lax.broadcasted_iota(jnp.int32, (tq, tq), 1)).astype(BF16)

    def sweep_block(kb, masked):
        k0 = pl.multiple_of(kb * tq, tq)
        z = jnp.concatenate(
            [_dot_nt(q_pairs[p], k_ref[pl.ds(k0, tq), p * pw:(p + 1) * pw]) for p in range(n_pairs)],
            axis=0)
        soft = jnp.log2(1.0 + jnp.exp2(jnp.minimum(z, -z)))
        log_beta = jnp.minimum(z, 0.0) - soft
        log_rem = log_beta - z
        if masked:
            log_rem = jnp.where(diag_mask, log_rem, 0.0)
        between = jnp.dot(log_rem.astype(BF16), later, preferred_element_type=F32)
        if masked:
            w = jnp.exp2(jnp.where(diag_mask, log_beta + between, -jnp.inf)).astype(BF16)
            rem = jnp.broadcast_to(jnp.sum(log_rem, axis=-1, keepdims=True), rem_ref.shape)
        else:
            rem = rem_ref[...]
            w = jnp.exp2(log_beta + between + jnp.concatenate([rem] * (tq // rem.shape[1]), axis=1)).astype(BF16)
            rem = rem + jnp.sum(log_rem, axis=-1, keepdims=True)
        rem_ref[...] = rem
        for p in range(n_pairs):
            rows = slice(2 * p * tq, 2 * (p + 1) * tq)
            pv = jnp.dot(w[rows, :], v_ref[pl.ds(k0, tq), p * pw:(p + 1) * pw], preferred_element_type=F32)
            if masked:
                acc_ref[rows, :] = pv
            else:
                acc_ref[rows, :] += pv
        return jnp.max(rem)

    def cond(carry):
        kb, rem_max = carry
        return jnp.logical_and(kb >= 0, rem_max > EXP2_ZERO_BELOW)

    def body(carry):
        kb, _ = carry
        return kb - 1, sweep_block(kb, masked=False)

    lax.while_loop(cond, body, (i - 1, sweep_block(i, masked=True)))
    for p in range(n_pairs):
        first = acc_ref[2 * p * tq:(2 * p + 1) * tq, :]
        second = acc_ref[(2 * p + 1) * tq:(2 * p + 2) * tq, :]
        o_ref[:, p * pw:(p + 1) * pw] = jnp.where(first_head, first, second).astype(o_ref.dtype)


def stick_breaking(q, kv, bsz, seq, tq=256, n_pairs=2):
    n = bsz * seq
    nq = seq // tq
    groups = SB_HEADS // (2 * n_pairs)
    w = 2 * SB_HEAD_DIM * n_pairs
    return pl.pallas_call(
        functools.partial(_sb_kernel, tq=tq, n_pairs=n_pairs),
        grid=(bsz, groups, nq),
        in_specs=[
            pl.BlockSpec((tq, w), lambda b, p, i: (b * nq + i, p)),
            pl.BlockSpec((seq, w), lambda b, p, i: (b, p)),
            pl.BlockSpec((seq, w), lambda b, p, i: (b, groups + p)),
        ],
        out_specs=pl.BlockSpec((tq, w), lambda b, p, i: (b * nq + i, p)),
        out_shape=jax.ShapeDtypeStruct((n, D_MODEL), BF16),
        scratch_shapes=[pltpu.VMEM((2 * n_pairs * tq, 2 * SB_HEAD_DIM), F32),
                        pltpu.VMEM((2 * n_pairs * tq, 2 * SB_HEAD_DIM), F32)],
        compiler_params=_cparams(("parallel", "parallel", "arbitrary")),
        name="stick_breaking",
    )(q, kv, kv)


def _swiglu_kernel(x_ref, nw_ref, wg_ref, wu_ref, wd_ref, o_ref, h_ref):
    @pl.when(pl.program_id(1) == 0)
    def _():
        x = x_ref[...]
        h_ref[...] = _rmsnorm_rows(x, nw_ref[...]).astype(BF16)
        o_ref[...] = x

    h = h_ref[...]
    gate = jnp.dot(h, wg_ref[...], preferred_element_type=F32)
    up = jnp.dot(h, wu_ref[...], preferred_element_type=F32)
    act = (gate * _sigmoid(gate) * up).astype(BF16)
    o_ref[...] += jnp.dot(act, wd_ref[...], preferred_element_type=F32)


def swiglu_residual(x, norm_w, w_gu_bf16, w_down_bf16, tm=1024, tf=1408):
    n, d = x.shape
    f = w_down_bf16.shape[0]
    nf = f // tf
    return pl.pallas_call(
        _swiglu_kernel,
        grid=(n // tm, nf),
        in_specs=[
            pl.BlockSpec((tm, d), lambda i, j: (i, 0)),
            pl.BlockSpec((1, d), lambda i, j: (0, 0)),
            pl.BlockSpec((d, tf), lambda i, j: (0, j)),
            pl.BlockSpec((d, tf), lambda i, j: (0, nf + j)),
            pl.BlockSpec((tf, d), lambda i, j: (j, 0)),
        ],
        out_specs=pl.BlockSpec((tm, d), lambda i, j: (i, 0)),
        out_shape=jax.ShapeDtypeStruct((n, d), F32),
        scratch_shapes=[pltpu.VMEM((tm, d), BF16)],
        compiler_params=_cparams(("parallel", "arbitrary")),
        name="swiglu_residual",
    )(x, norm_w.reshape(1, d), w_gu_bf16, w_gu_bf16, w_down_bf16)


ROUTER_LANES = 128


def _split_bf16(a):
    hi = a.astype(BF16)
    return hi, (a - hi.astype(F32)).astype(BF16)


def _router_kernel(x_ref, nw_ref, r_ref, idx_ref, gate_ref):
    h_hi, h_lo = _split_bf16(_rmsnorm_rows(x_ref[...], nw_ref[...]))
    r_hi, r_lo = _split_bf16(r_ref[...])
    logits = (jnp.dot(h_hi, r_hi, preferred_element_type=F32) + jnp.dot(h_hi, r_lo, preferred_element_type=F32)
              + jnp.dot(h_lo, r_hi, preferred_element_type=F32))
    lane = lax.broadcasted_iota(jnp.int32, logits.shape, 1)
    lane_f = lane.astype(F32)
    logits = jnp.where(lane < N_EXPERTS, logits, -jnp.inf)
    m1 = jnp.max(logits, axis=-1, keepdims=True)
    i1 = jnp.min(jnp.where(logits == m1, lane_f, float(ROUTER_LANES)), axis=-1, keepdims=True)
    rest = jnp.where(lane_f == i1, -jnp.inf, logits)
    m2 = jnp.max(rest, axis=-1, keepdims=True)
    i2 = jnp.min(jnp.where(rest == m2, lane_f, float(ROUTER_LANES)), axis=-1, keepdims=True)
    e = jnp.exp(m2 - m1)
    g1 = 1.0 / (1.0 + e)
    first = lane == 0
    idx_ref[...] = jnp.where(first, i1, i2)[:, :TOP_K].astype(jnp.int32)
    gate_ref[...] = jnp.where(first, g1, e * g1)[:, :TOP_K]


def router_top2(x, norm_w, router, tm=1024):
    n, d = x.shape
    r_pad = jnp.zeros((d, ROUTER_LANES), F32).at[:, :N_EXPERTS].set(router)
    return pl.pallas_call(
        _router_kernel,
        grid=(n // tm,),
        in_specs=[
            pl.BlockSpec((tm, d), lambda i: (i, 0)),
            pl.BlockSpec((1, d), lambda i: (0, 0)),
            pl.BlockSpec((d, ROUTER_LANES), lambda i: (0, 0)),
        ],
        out_specs=[pl.BlockSpec((tm, TOP_K), lambda i: (i, 0)), pl.BlockSpec((tm, TOP_K), lambda i: (i, 0))],
        out_shape=[jax.ShapeDtypeStruct((n, TOP_K), jnp.int32), jax.ShapeDtypeStruct((n, TOP_K), F32)],
        compiler_params=_cparams(("parallel",)),
        name="moe_router",
    )(x, norm_w.reshape(1, d), r_pad)


def _moe_kernel(be_ref, nv_ref, nu_ref, tok_ref, tok_next_ref, dst_ref, x_hbm, nw_ref, wg_ref, wu_ref, wd_ref,
                out_hbm, xbuf, hbuf, acc_ref, ybuf, gsem, ssem, *, tm):
    i = pl.program_id(0)
    j = pl.program_id(1)
    last_j = pl.num_programs(1) - 1
    n_used = nu_ref[0]
    used = i < n_used

    def gather_copy(tok, slot, g, u):
        return pltpu.make_async_copy(x_hbm.at[pl.ds(tok, 1), :], xbuf.at[slot, g, pl.ds(u, 1), :], gsem.at[slot])

    def scatter_copy(dst, g, u):
        return pltpu.make_async_copy(ybuf.at[g, pl.ds(u, 1), :], out_hbm.at[pl.ds(dst, 1), :], ssem)

    def for_rows(count, issue):
        n_groups = lax.shift_right_logical(count, 3)

        def group(g, c):
            for u in range(8):
                issue(g * 8 + u, g, u, u % 2)
            return c

        def single(r, c):
            issue(r, n_groups, r - n_groups * 8, 0)
            return c

        lax.fori_loop(0, n_groups, group, 0)
        lax.fori_loop(n_groups * 8, count, single, 0)

    def start_gather(idx_ref, slot, count):
        for_rows(count, lambda r, g, u, prio: gather_copy(idx_ref[0, 0, r], slot, g, u).start(priority=prio))

    def wait_rows(buf, sem, count):
        p = tm
        while p >= 1:
            part = buf.at[pl.ds(0, p // 8)] if p >= 8 else buf.at[0, pl.ds(0, p), :]

            @pl.when((count & p) != 0)
            def _(part=part):
                pltpu.make_async_copy(part, part, sem).wait()

            p //= 2

    @pl.when(jnp.logical_and(used, j == 0))
    def _():
        slot = lax.rem(i, 2)

        @pl.when(i == 0)
        def _():
            xbuf[...] = jnp.zeros_like(xbuf)
            start_gather(tok_ref, 0, nv_ref[0])

        wait_rows(xbuf.at[slot], gsem.at[slot], nv_ref[i])

        @pl.when(i + 1 < n_used)
        def _():
            start_gather(tok_next_ref, 1 - slot, nv_ref[i + 1])

        x = xbuf[slot].reshape(tm, xbuf.shape[-1])
        hbuf[...] = _rmsnorm_rows(x, nw_ref[...]).astype(BF16)

    @pl.when(used)
    def _():
        h = hbuf[...]
        gate = jnp.dot(h, wg_ref[0, 0].astype(BF16), preferred_element_type=F32)
        up = jnp.dot(h, wu_ref[0, 0].astype(BF16), preferred_element_type=F32)
        act = (gate * _sigmoid(gate) * up).astype(BF16)
        part = jnp.dot(act, wd_ref[0, 0].astype(BF16), preferred_element_type=F32)

        @pl.when(j == 0)
        def _():
            acc_ref[...] = part

        @pl.when(jnp.logical_and(j > 0, j < last_j))
        def _():
            acc_ref[...] += part

        @pl.when(j == last_j)
        def _():
            @pl.when(i > 0)
            def _():
                wait_rows(ybuf, ssem, nv_ref[i - 1])

            ybuf[...] = (acc_ref[...] + part).reshape(ybuf.shape)

            for_rows(nv_ref[i], lambda r, g, u, prio: scatter_copy(dst_ref[0, 0, r], g, u).start(priority=prio))

            @pl.when(i == n_used - 1)
            def _():
                wait_rows(ybuf, ssem, nv_ref[i])


def moe_experts(x, norm_w, tok, dst, block_expert, n_valid, n_used, wg, wu, wd, moe_idx, tm, tf=512):
    n, d = x.shape
    f = wg.shape[3]
    n_blocks = tok.shape[0]
    nj = f // tf

    def j_eff(i, j, nu):
        return jnp.where(i < nu[0], j, nj - 1)

    grid_spec = pltpu.PrefetchScalarGridSpec(
        num_scalar_prefetch=3,
        grid=(n_blocks, nj),
        in_specs=[
            pl.BlockSpec((1, 1, tm), lambda i, j, be, nv, nu: (i, 0, 0), memory_space=pltpu.SMEM),
            pl.BlockSpec((1, 1, tm), lambda i, j, be, nv, nu: (jnp.minimum(i + 1, n_blocks - 1), 0, 0),
                         memory_space=pltpu.SMEM),
            pl.BlockSpec((1, 1, tm), lambda i, j, be, nv, nu: (i, 0, 0), memory_space=pltpu.SMEM),
            pl.BlockSpec(memory_space=pl.ANY),
            pl.BlockSpec((1, d), lambda i, j, be, nv, nu: (0, 0)),
            pl.BlockSpec((1, 1, d, tf), lambda i, j, be, nv, nu: (moe_idx, be[i], 0, j_eff(i, j, nu))),
            pl.BlockSpec((1, 1, d, tf), lambda i, j, be, nv, nu: (moe_idx, be[i], 0, j_eff(i, j, nu))),
            pl.BlockSpec((1, 1, tf, d), lambda i, j, be, nv, nu: (moe_idx, be[i], j_eff(i, j, nu), 0)),
        ],
        out_specs=pl.BlockSpec(memory_space=pl.ANY),
        scratch_shapes=[
            pltpu.VMEM((2, tm // 8, 8, d), F32),
            pltpu.VMEM((tm, d), BF16),
            pltpu.VMEM((tm, d), F32),
            pltpu.VMEM((tm // 8, 8, d), F32),
            pltpu.SemaphoreType.DMA((2,)),
            pltpu.SemaphoreType.DMA(()),
        ],
    )
    return pl.pallas_call(
        functools.partial(_moe_kernel, tm=tm),
        grid_spec=grid_spec,
        out_shape=jax.ShapeDtypeStruct((TOP_K * n, d), F32),
        compiler_params=_cparams(("arbitrary", "arbitrary")),
        name="moe_experts",
    )(block_expert, n_valid, n_used, tok, tok, dst, x, norm_w.reshape(1, d), wg, wu, wd)


def _combine_kernel(x_ref, y0_ref, y1_ref, g_ref, o_ref):
    g = g_ref[...]
    o_ref[...] = x_ref[...] + g[:, 0:1] * y0_ref[0] + g[:, 1:2] * y1_ref[0]


def _combine_norm_kernel(x_ref, y0_ref, y1_ref, g_ref, w_ref, o_ref):
    g = g_ref[...]
    o_ref[...] = _rmsnorm_rows(x_ref[...] + g[:, 0:1] * y0_ref[0] + g[:, 1:2] * y1_ref[0], w_ref[...])


def moe_combine(x, y, gates, out_norm_w=None, tm=512):
    n, d = x.shape
    in_specs = [pl.BlockSpec((tm, d), lambda i: (i, 0)),
                pl.BlockSpec((1, tm, d), lambda i: (0, i, 0)),
                pl.BlockSpec((1, tm, d), lambda i: (1, i, 0)),
                pl.BlockSpec((tm, TOP_K), lambda i: (i, 0))]
    args = [x, y, y, gates]
    body = _combine_kernel
    if out_norm_w is not None:
        in_specs.append(pl.BlockSpec((1, d), lambda i: (0, 0)))
        args.append(out_norm_w.reshape(1, d))
        body = _combine_norm_kernel
    return pl.pallas_call(
        body,
        grid=(n // tm,),
        in_specs=in_specs,
        out_specs=pl.BlockSpec((tm, d), lambda i: (i, 0)),
        out_shape=jax.ShapeDtypeStruct((n, d), F32),
        compiler_params=_cparams(("parallel",)),
        name="moe_combine",
    )(*args)


def moe_layer(x, norm_w, router, wg, wu, wd, moe_idx, out_norm_w=None, tm=1024):
    n, d = x.shape
    n_assign = n * TOP_K
    top_idx, gates = router_top2(x, norm_w, router)

    flat_e = top_idx.reshape(-1)
    order = jnp.argsort(flat_e).astype(jnp.int32)
    counts = jnp.sum((flat_e[:, None] == jnp.arange(N_EXPERTS, dtype=jnp.int32)[None, :]).astype(jnp.int32), axis=0)
    starts = jnp.cumsum(counts) - counts
    n_blk_e = (counts + tm - 1) // tm
    blk_ends = jnp.cumsum(n_blk_e)
    blk_starts = blk_ends - n_blk_e
    n_blocks = n_assign // tm + N_EXPERTS
    blk = jnp.arange(n_blocks, dtype=jnp.int32)
    block_expert = jnp.minimum(jnp.sum((blk[:, None] >= blk_ends[None, :]).astype(jnp.int32), axis=1),
                               N_EXPERTS - 1)
    n_used = blk_ends[-1].astype(jnp.int32).reshape(1)
    row0 = (blk - blk_starts[block_expert]) * tm
    n_valid = jnp.where(blk < n_used[0], jnp.clip(counts[block_expert] - row0, 0, tm), 0).astype(jnp.int32)
    pos = starts[block_expert][:, None] + row0[:, None] + jnp.arange(tm, dtype=jnp.int32)[None, :]
    asg = jnp.take(order, jnp.clip(pos, 0, n_assign - 1), axis=0).reshape(n_blocks, 1, tm)
    tok = lax.shift_right_logical(asg, 1)
    dst = (asg & 1) * n + tok

    y = moe_experts(x, norm_w, tok, dst, block_expert, n_valid, n_used, wg, wu, wd, moe_idx, tm)
    return moe_combine(x, y.reshape(TOP_K, n, d), gates, out_norm_w)


def kernel(x, mix_norm, ffn_norm, hgrn_w_in, hgrn_lb_raw, hgrn_gnorm, hgrn_w_out, kv_norm, w_kv, sb_w_q, sb_w_o, ffn_w_gu, ffn_w_down, moe_router, moe_w_gate, moe_w_up, moe_w_down, final_norm):
    bsz, seq, d = x.shape
    n = bsz * seq
    depth = mix_norm.shape[0]
    n_a = hgrn_w_in.shape[0]
    assert depth % 2 == 0, "the final rmsnorm is fused into the last (expert) layer's combine"
    xs = x.reshape(n, d)

    p = jax.nn.softmax(hgrn_lb_raw.astype(F32), axis=0)
    lower_bounds = jnp.cumsum(p, axis=0) - p[0:1]

    kv = None
    for layer in range(depth):
        if layer < n_a:
            qig, f_raw = hgrn_in_proj(xs, mix_norm[layer], hgrn_w_in, layer)
            og = hgrn_recurrence(qig, f_raw, lower_bounds[layer], hgrn_gnorm[layer], bsz, seq)
            xs = matmul_residual(og, hgrn_w_out, layer, xs)
        else:
            j = layer - n_a
            q = norm_matmul(xs, mix_norm[layer], sb_w_q, j, BF16, out_scale=SB_LOGIT_SCALE_LOG2)
            att = stick_breaking(q, kv, bsz, seq)
            xs = matmul_residual(att, sb_w_o, j, xs)
        if layer % 2 == 0:
            xs = swiglu_residual(xs, ffn_norm[layer], ffn_w_gu[layer // 2].astype(BF16),
                                 ffn_w_down[layer // 2].astype(BF16))
        else:
            e = layer // 2
            xs = moe_layer(xs, ffn_norm[layer], moe_router[e], moe_w_gate, moe_w_up, moe_w_down, e,
                           out_norm_w=final_norm if layer == depth - 1 else None)
        if layer == n_a - 1:
            kv = norm_matmul(xs, kv_norm, w_kv.reshape(1, d, 2 * d), 0, BF16, tn=2 * d)
    return xs.reshape(bsz, seq, d)
```
